```python
import math
import jax, jax.numpy as jnp
from jax import lax
import numpy as np

D_MODEL = 1024
BATCH = 8
SEQ = 4096
DEPTH = 1

CHUNK = 64
Q_BLOCK = 128
A_HEADS = 8
A_HEAD_DIM = 64
A_WIDTH = A_HEADS * A_HEAD_DIM
DECAY_LORA = 64
AAA_LORA = 64
GATE_LORA = 128
A_IN = 3 * A_WIDTH + DECAY_LORA + AAA_LORA + GATE_LORA
B_HEADS = 4
B_HEAD_DIM = 64
B_WIDTH = B_HEADS * 2 * B_HEAD_DIM
B_IN = 3 * B_WIDTH
MIX_WIDTH = A_WIDTH + B_WIDTH
IN_WIDTH = A_IN + B_IN
D_FF = 2816
CONV_WIDTH = 3
NORM_EPS = 1e-6
LNX_EPS = 64e-5
DECAY_SCALE = math.exp(-0.5)
L2_EPS = 1e-12

kernel_name = "hymba_rwkv7_diffattn_convffn_block"


def _rms_norm(x, w, eps=NORM_EPS):
    xf = x.astype(jnp.float32)
    y = xf * lax.rsqrt(jnp.mean(xf * xf, axis=-1, keepdims=True) + eps)
    return (y * w.astype(jnp.float32)).astype(x.dtype)


def _token_shift(h):
    return jnp.pad(h, ((0, 0), (1, 0), (0, 0)))[:, :-1]


def _rwkv7_scan(r, w, k, v, a, b):
    bsz, _, nh, n = r.shape

    def step(S, inp):
        r_t, w_t, k_t, v_t, a_t, b_t = inp
        sa = jnp.einsum('bhvk,bhk->bhv', S, a_t)
        S = (S * w_t[:, :, None, :] + sa[..., None] * b_t[:, :, None, :]
             + v_t[..., None] * k_t[:, :, None, :])
        return S, jnp.einsum('bhvk,bhk->bhv', S, r_t)

    seq = (jnp.moveaxis(r, 1, 0), jnp.moveaxis(w, 1, 0), jnp.moveaxis(k, 1, 0),
           jnp.moveaxis(v, 1, 0), jnp.moveaxis(a, 1, 0), jnp.moveaxis(b, 1, 0))
    S0 = jnp.zeros((bsz, nh, n, n), jnp.float32)
    _, y = lax.scan(step, S0, seq)
    return jnp.moveaxis(y, 0, 1)


def _rwkv7_mixer(h, mu, w0, w_decay_up, a0, w_aaa_up, w_gate_up, k_k, k_a, r_k, ln_x_w, ln_x_b):
    bsz, T, _ = h.shape
    h = h + (_token_shift(h) - h) * mu
    s1, s2, s3 = A_WIDTH, 2 * A_WIDTH, 3 * A_WIDTH
    r, k, v, wd, ad, gd = jnp.split(h, [s1, s2, s3, s3 + DECAY_LORA, s3 + DECAY_LORA + AAA_LORA], axis=-1)
    w = jnp.exp(-DECAY_SCALE * jax.nn.sigmoid((w0 + jnp.tanh(wd) @ w_decay_up).astype(jnp.float32)))
    a = jax.nn.sigmoid(a0 + ad @ w_aaa_up)
    g = jax.nn.sigmoid(gd) @ w_gate_up

    def heads(t):
        return t.reshape(bsz, T, A_HEADS, A_HEAD_DIM).astype(jnp.float32)

    kk = heads(k * k_k)
    kk = kk * lax.rsqrt(jnp.sum(kk * kk, axis=-1, keepdims=True) + L2_EPS)
    k_h = heads(k * (1.0 + (a - 1.0) * k_a))
    r_h, v_h, a_h, w_h = heads(r), heads(v), heads(a), heads(w)
    y = _rwkv7_scan(r_h, w_h, k_h, v_h, -kk, kk * a_h)
    mean = jnp.mean(y, axis=-1, keepdims=True)
    var = jnp.mean(jnp.square(y - mean), axis=-1, keepdims=True)
    y = ((y - mean) * lax.rsqrt(var + LNX_EPS)).reshape(bsz, T, A_WIDTH)
    y = y * ln_x_w.astype(jnp.float32) + ln_x_b.astype(jnp.float32)
    bonus = jnp.sum(r_h * k_h * r_k.astype(jnp.float32), axis=-1, keepdims=True) * v_h
    y = y + bonus.reshape(bsz, T, A_WIDTH)
    return (y * g.astype(jnp.float32)).astype(h.dtype)


def _diff_attention(h, q_norm_w, k_norm_w, lambda_q1, lambda_k1, lambda_q2, lambda_k2, subln_w, lambda_init):
    bsz, T, _ = h.shape
    nb = T // Q_BLOCK
    q, k, v = jnp.split(h, 3, axis=-1)
    q = _rms_norm(q.reshape(bsz, T, B_HEADS, 2, B_HEAD_DIM), q_norm_w)
    k = _rms_norm(k.reshape(bsz, T, B_HEADS, 2, B_HEAD_DIM), k_norm_w)
    v = v.reshape(bsz, T, B_HEADS, 2 * B_HEAD_DIM)
    lam = (jnp.exp(jnp.sum(lambda_q1 * lambda_k1).astype(jnp.float32))
           - jnp.exp(jnp.sum(lambda_q2 * lambda_k2).astype(jnp.float32)) + lambda_init)
    slopes = jnp.exp2(-8.0 * jnp.arange(1, B_HEADS + 1, dtype=jnp.float32) / B_HEADS)
    scale = 1.0 / math.sqrt(B_HEAD_DIM)
    q_blocks = q.reshape(bsz, nb, Q_BLOCK, B_HEADS, 2, B_HEAD_DIM).transpose(1, 0, 3, 4, 2, 5)
    k_t = k.transpose(0, 2, 3, 1, 4)
    v_t = v.transpose(0, 2, 1, 3)
    key_pos = jnp.arange(T)

    def block(args):
        i, q_blk = args
        q_pos = i * Q_BLOCK + jnp.arange(Q_BLOCK)
        s = jnp.einsum('bhmqd,bhmkd->bhmqk', q_blk, k_t).astype(jnp.float32) * scale
        dist = jnp.abs(q_pos[:, None] - key_pos[None, :]).astype(jnp.float32)
        bias = -slopes[:, None, None, None] * dist[None, None]
        allowed = (key_pos[None, :] // CHUNK) <= (q_pos[:, None] // CHUNK)
        s = jnp.where(allowed, s + bias, -jnp.inf)
        p = jax.nn.softmax(s, axis=-1)
        attn = p[:, :, 0] - lam * p[:, :, 1]
        return jnp.einsum('bhqk,bhkd->bhqd', attn.astype(v_t.dtype), v_t)

    o = lax.map(block, (jnp.arange(nb), q_blocks))
    o = o.transpose(1, 0, 3, 2, 4).reshape(bsz, T, B_HEADS, 2 * B_HEAD_DIM)
    o = _rms_norm(o, subln_w) * (1.0 - lambda_init)
    return o.reshape(bsz, T, B_WIDTH)


def _conv_ffn(h, w_up, conv_w, conv_b, w_down):
    u = h @ w_up
    u = lax.conv_general_dilated(
        u, conv_w[:, None, :].astype(u.dtype), window_strides=(1,),
        padding=[(CONV_WIDTH - 1, 0)], dimension_numbers=('NWC', 'WIO', 'NWC'),
        feature_group_count=2 * D_FF) + conv_b
    gate, up = jnp.split(u, 2, axis=-1)
    return (jax.nn.silu(gate) * up) @ w_down


def setup_inputs(seed: int = 0) -> dict:
    key = jax.random.key(seed)
    ks = jax.random.split(key, 32)
    L = DEPTH

    def nrm(k, shape, s):
        return s * jax.random.normal(k, shape, jnp.float32)

    return {
        "x": nrm(ks[0], (BATCH, SEQ, D_MODEL), 1.0),
        "attn_norm_w": 1.0 + nrm(ks[1], (L, D_MODEL), 0.02),
        "w_in": nrm(ks[2], (L, D_MODEL, IN_WIDTH), D_MODEL ** -0.5),
        "mu_shift": jax.random.uniform(ks[3], (L, A_IN), jnp.float32),
        "w0": nrm(ks[4], (L, A_WIDTH), 1.0),
        "w_decay_up": nrm(ks[5], (L, DECAY_LORA, A_WIDTH), 0.5 * DECAY_LORA ** -0.5),
        "a0": nrm(ks[6], (L, A_WIDTH), 0.5),
        "w_aaa_up": nrm(ks[7], (L, AAA_LORA, A_WIDTH), 0.5 * AAA_LORA ** -0.5),
        "w_gate_up": nrm(ks[8], (L, GATE_LORA, A_WIDTH), GATE_LORA ** -0.5),
        "k_k": 0.85 + nrm(ks[9], (L, A_WIDTH), 0.02),
        "k_a": 1.0 + nrm(ks[10], (L, A_WIDTH), 0.02),
        "r_k": nrm(ks[11], (L, A_HEADS, A_HEAD_DIM), 0.1),
        "ln_x_w": 1.0 + nrm(ks[12], (L, A_WIDTH), 0.02),
        "ln_x_b": nrm(ks[13], (L, A_WIDTH), 0.02),
        "q_norm_w": 1.0 + nrm(ks[14], (L, B_HEAD_DIM), 0.02),
        "k_norm_w": 1.0 + nrm(ks[15], (L, B_HEAD_DIM), 0.02),
        "lambda_q1": nrm(ks[16], (L, B_HEAD_DIM), 0.1),
        "lambda_k1": nrm(ks[17], (L, B_HEAD_DIM), 0.1),
        "lambda_q2": nrm(ks[18], (L, B_HEAD_DIM), 0.1),
        "lambda_k2": nrm(ks[19], (L, B_HEAD_DIM), 0.1),
        "subln_w": 1.0 + nrm(ks[20], (L, 2 * B_HEAD_DIM), 0.02),
        "w_out": nrm(ks[21], (L, MIX_WIDTH, D_MODEL), MIX_WIDTH ** -0.5),
        "ffn_norm_w": 1.0 + nrm(ks[22], (L, D_MODEL), 0.02),
        "w_ffn_up": nrm(ks[23], (L, D_MODEL, 2 * D_FF), D_MODEL ** -0.5),
        "ffn_conv_w": nrm(ks[24], (L, CONV_WIDTH, 2 * D_FF), CONV_WIDTH ** -0.5),
        "ffn_conv_b": nrm(ks[25], (L, 2 * D_FF), 0.02),
        "w_ffn_down": nrm(ks[26], (L, D_FF, D_MODEL), D_FF ** -0.5),
    }


def reference(x, attn_norm_w, w_in, mu_shift, w0, w_decay_up, a0, w_aaa_up, w_gate_up,
              k_k, k_a, r_k, ln_x_w, ln_x_b, q_norm_w, k_norm_w, lambda_q1, lambda_k1,
              lambda_q2, lambda_k2, subln_w, w_out, ffn_norm_w, w_ffn_up, ffn_conv_w,
              ffn_conv_b, w_ffn_down):
    for l in range(DEPTH):
        lambda_init = 0.8 - 0.6 * math.exp(-0.3 * l)
        h = _rms_norm(x, attn_norm_w[l])
        z = h @ w_in[l]
        y_a = _rwkv7_mixer(z[..., :A_IN], mu_shift[l], w0[l], w_decay_up[l], a0[l], w_aaa_up[l],
                           w_gate_up[l], k_k[l], k_a[l], r_k[l], ln_x_w[l], ln_x_b[l])
        y_b = _diff_attention(z[..., A_IN:], q_norm_w[l], k_norm_w[l], lambda_q1[l], lambda_k1[l],
                              lambda_q2[l], lambda_k2[l], subln_w[l], lambda_init)
        x = x + jnp.concatenate([y_a, y_b], axis=-1) @ w_out[l]
        x = x + _conv_ffn(_rms_norm(x, ffn_norm_w[l]), w_ffn_up[l], ffn_conv_w[l],
                          ffn_conv_b[l], w_ffn_down[l])
    return x
```

```python
import functools
import math

import jax
import jax.numpy as jnp
from jax import lax
from jax.experimental import pallas as pl
from jax.experimental.pallas import tpu as pltpu

f32 = jnp.float32
bf16 = jnp.bfloat16

D_MODEL = 1024
CHUNK = 64
A_HEADS = 8
A_HEAD_DIM = 64
A_WIDTH = A_HEADS * A_HEAD_DIM
DECAY_LORA = 64
AAA_LORA = 64
GATE_LORA = 128
A_IN = 3 * A_WIDTH + DECAY_LORA + AAA_LORA + GATE_LORA
B_HEADS = 4
B_HEAD_DIM = 64
B_WIDTH = B_HEADS * 2 * B_HEAD_DIM
B_IN = 3 * B_WIDTH
IN_WIDTH = A_IN + B_IN
D_FF = 2816
NORM_EPS = 1e-6
LNX_EPS = 64e-5
DECAY_SCALE = math.exp(-0.5)
L2_EPS = 1e-12

LANES = 128
MXU_DIM = 256
VMEM_LIMIT = 56 * 1024 * 1024

IN_TM = 512
SCAN_T = 256
ATT_BQ = 256
FFN_TM = 512
FFN_TF = 256
NEG_BIG = -1e30


def _dot(a, b, dims=((1,), (0,))):
    return lax.dot_general(a, b, (dims, ((), ())), preferred_element_type=f32)


def _dot_nt(a, b):
    return _dot(a, b, ((1,), (1,)))


def _split2(x):
    hi = x.astype(bf16)
    lo = (x - hi.astype(f32)).astype(bf16)
    return hi, lo


def _split3(x):
    hi = x.astype(bf16)
    r1 = x - hi.astype(f32)
    mid = r1.astype(bf16)
    lo = (r1 - mid.astype(f32)).astype(bf16)
    return hi, mid, lo


def _dot_x3(a, b, dims=((1,), (0,))):
    ah, al = _split2(a)
    bh, bl = _split2(b)
    return _dot(ah, bh, dims) + (_dot(ah, bl, dims) + _dot(al, bh, dims))


def _group_sum(x, ones_bd):
    hi, lo = _split2(x)
    return _dot(hi, ones_bd) + _dot(lo, ones_bd)


def _const_spec(shape):
    nd = len(shape)
    return pl.BlockSpec(shape, lambda *_: (0,) * nd, pipeline_mode=pl.Buffered(1))


def _inproj_body(x_ref, nw_ref, w_ref, qkw_ref, ones_ref, za_ref, q_ref, k_ref, v_ref):
    x = x_ref[...]
    ms = jnp.mean(x * x, axis=-1, keepdims=True)
    h = (x * lax.rsqrt(ms + NORM_EPS) * nw_ref[...]).astype(bf16)
    za_ref[...] = _dot(h, w_ref[:, :A_IN])
    ones_bd = ones_ref[...]

    def qk_norm(col0, row):
        z = _dot(h, w_ref[:, col0:col0 + B_WIDTH])
        ss = _group_sum(z * z, ones_bd)
        return (z * lax.rsqrt(ss * (1.0 / B_HEAD_DIM) + NORM_EPS) * qkw_ref[row:row + 1, :]).astype(bf16)

    q_ref[...] = qk_norm(A_IN, 0)
    k_ref[...] = qk_norm(A_IN + B_WIDTH, 1)
    v_ref[...] = _dot(h, w_ref[:, A_IN + 2 * B_WIDTH:]).astype(bf16)


def _inproj(x2d, norm_w, w_in_bf, qk_w, ones_bd):
    n = x2d.shape[0]
    tm = min(IN_TM, n)
    row = lambda i: (i, 0)
    return pl.pallas_call(
        _inproj_body,
        grid=(n // tm,),
        in_specs=[
            pl.BlockSpec((tm, D_MODEL), row),
            _const_spec((1, D_MODEL)),
            _const_spec((D_MODEL, IN_WIDTH)),
            _const_spec((2, B_WIDTH)),
            _const_spec((B_WIDTH, B_WIDTH)),
        ],
        out_specs=[
            pl.BlockSpec((tm, A_IN), row),
            pl.BlockSpec((tm, B_WIDTH), row),
            pl.BlockSpec((tm, B_WIDTH), row),
            pl.BlockSpec((tm, B_WIDTH), row),
        ],
        out_shape=[
            jax.ShapeDtypeStruct((n, A_IN), f32),
            jax.ShapeDtypeStruct((n, B_WIDTH), bf16),
            jax.ShapeDtypeStruct((n, B_WIDTH), bf16),
            jax.ShapeDtypeStruct((n, B_WIDTH), bf16),
        ],
        compiler_params=pltpu.CompilerParams(
            dimension_semantics=("parallel",), vmem_limit_bytes=VMEM_LIMIT),
        name="inproj",
    )(x2d, norm_w, w_in_bf, qk_w, ones_bd)


def _apply_unit_lower_inverse(l_mat, x0):
    x = x0 + _dot(l_mat.astype(bf16), x0.astype(bf16))
    p = l_mat
    for _ in range(5):
        pb = p.astype(bf16)
        p = _dot(pb, pb)
        x = x + _dot(p.astype(bf16), x.astype(bf16))
    return x


def _rwkv_body(za_ref, halo_ref, mu_ref, w0_ref, a0_ref, wlora_ref, wgate_ref, kk_ref, ka_ref,
               rk_ref, lnw_ref, lnb_ref, ones_ref, out_ref, s_ref):
    tb = pl.program_id(1)
    tc = za_ref.shape[1]
    n_chunks = tc // CHUNK

    @pl.when(tb == 0)
    def _():
        s_ref[...] = jnp.zeros_like(s_ref)

    h = za_ref[0]
    last = jnp.where(tb == 0, 0.0, halo_ref[0][7:8, :])
    row0 = lax.broadcasted_iota(jnp.int32, (tc, 1), 0) == 0
    prev = jnp.where(row0, last, pltpu.roll(h, 1, axis=0))
    hs = h + (prev - h) * mu_ref[...]

    r = hs[:, 0:A_WIDTH]
    k = hs[:, A_WIDTH:2 * A_WIDTH]
    v = hs[:, 2 * A_WIDTH:3 * A_WIDTH]
    lora_in = hs[:, 3 * A_WIDTH:3 * A_WIDTH + LANES]
    gate_in = hs[:, 3 * A_WIDTH + LANES:]
    lane = lax.broadcasted_iota(jnp.int32, (1, LANES), 1)
    lora_act = jnp.where(lane < DECAY_LORA, jnp.tanh(lora_in), lora_in).astype(bf16)
    lora = _dot(lora_act, wlora_ref[...])
    logw = -DECAY_SCALE * jax.nn.sigmoid(w0_ref[...] + lora[:, :A_WIDTH])
    lr = jax.nn.sigmoid(a0_ref[...] + lora[:, A_WIDTH:])
    gate = _dot(jax.nn.sigmoid(gate_in).astype(bf16), wgate_ref[...])

    ones_bd = ones_ref[...]
    kk = k * kk_ref[...]
    kk = kk * lax.rsqrt(_group_sum(kk * kk, ones_bd) + L2_EPS)
    kh = k * (1.0 + (lr - 1.0) * ka_ref[...])
    a_s = -kk
    b_s = kk * lr
    bonus = _group_sum(r * kh * rk_ref[...], ones_bd) * v

    ri = lax.broadcasted_iota(jnp.int32, (tc, tc), 0)
    ci = lax.broadcasted_iota(jnp.int32, (tc, tc), 1)
    in_chunk = (ri & (CHUNK - 1)).astype(jnp.uint32)
    dist = (ri - ci).astype(jnp.uint32)
    incl = dist <= in_chunk
    strict = (dist - 1) < in_chunk
    tri = jnp.where(incl, 1.0, 0.0).astype(bf16)
    w_hi, w_mid, w_lo = _split3(logw)
    cs = _dot(tri, w_hi) + (_dot(tri, w_mid) + _dot(tri, w_lo))
    cs3 = cs.reshape(n_chunks, CHUNK, A_WIDTH)
    cs_end = jnp.broadcast_to(cs3[:, CHUNK - 1:CHUNK, :], cs3.shape).reshape(tc, A_WIDTH)

    e_neg = jnp.exp(-cs)
    e_end = jnp.exp(cs_end - cs)
    a_t = a_s * jnp.exp(cs - logw)
    r_t = r * jnp.exp(cs)
    b_t = b_s * e_neg
    k_t = kh * e_neg
    b_h = b_s * e_end
    k_h = kh * e_end
    g_end = jnp.exp(cs_end)

    w_parts, u_parts, rp_parts, yi_parts = [], [], [], []
    for p in range(A_HEADS // 2):
        sl = slice(p * LANES, (p + 1) * LANES)
        bk = jnp.concatenate([b_t[:, sl], k_t[:, sl]], axis=0).astype(bf16)
        w_p = u_p = rp_p = yi_p = None
        for hh in range(2):
            hm = (lane >= hh * A_HEAD_DIM) & (lane < (hh + 1) * A_HEAD_DIM)
            a_m = jnp.where(hm, a_t[:, sl], 0.0)
            r_m = jnp.where(hm, r_t[:, sl], 0.0)
            v_m = jnp.where(hm, v[:, sl], 0.0)
            ar = jnp.concatenate([a_m, r_m], axis=0).astype(bf16)
            sc = _dot_nt(ar, bk)
            l_ab = jnp.where(strict, sc[:tc, :tc], 0.0)
            l_ak = jnp.where(strict, sc[:tc, tc:], 0.0)
            m_rb = jnp.where(incl, sc[tc:, :tc], 0.0)
            m_rk = jnp.where(incl, sc[tc:, tc:], 0.0)
            v_mb = v_m.astype(bf16)
            x0 = jnp.concatenate([a_m, _dot(l_ak.astype(bf16), v_mb)], axis=1)
            x = _apply_unit_lower_inverse(l_ab, x0)
            w_m = x[:, :LANES]
            u_m = x[:, LANES:]
            m_rb_b = m_rb.astype(bf16)
            rp_m = r_m + _dot(m_rb_b, w_m.astype(bf16))
            yi_m = _dot(m_rb_b, u_m.astype(bf16)) + _dot(m_rk.astype(bf16), v_mb)
            if hh == 0:
                w_p, u_p, rp_p, yi_p = w_m, u_m, rp_m, yi_m
            else:
                w_p, u_p, rp_p, yi_p = w_p + w_m, u_p + u_m, rp_p + rp_m, yi_p + yi_m
        w_parts.append(w_p)
        u_parts.append(u_p)
        rp_parts.append(rp_p)
        yi_parts.append(yi_p)

    qi = lax.broadcasted_iota(jnp.int32, (MXU_DIM, MXU_DIM), 0)
    qj = lax.broadcasted_iota(jnp.int32, (MXU_DIM, MXU_DIM), 1)
    same_head = (qi ^ qj) < A_HEAD_DIM
    eye = qi == qj
    n_groups = A_WIDTH // MXU_DIM
    y_groups = []
    for g in range(n_groups):
        gs = slice(g * MXU_DIM, (g + 1) * MXU_DIM)
        cat = lambda parts: jnp.concatenate(parts[2 * g:2 * g + 2], axis=1)
        w_g, u_g, rp_g, yi_g = cat(w_parts), cat(u_parts), cat(rp_parts), cat(yi_parts)
        y_rows = []
        for c in range(n_chunks):
            rows = slice(c * CHUNK, (c + 1) * CHUNK)
            bh_c = b_h[rows, gs]
            pt = jnp.where(same_head, _dot(w_g[rows].T.astype(bf16), bh_c.astype(bf16)), 0.0)
            pt = pt + jnp.where(eye, g_end[c * CHUNK:c * CHUNK + 1, gs], 0.0)
            uv = jnp.concatenate([u_g[rows], v[rows, gs]], axis=0)
            bkh = jnp.concatenate([bh_c, k_h[rows, gs]], axis=0)
            qt = jnp.where(same_head, _dot(uv.T.astype(bf16), bkh.astype(bf16)), 0.0)
            s = s_ref[g]
            y_rows.append(_dot_x3(rp_g[rows], s, ((1,), (1,))) + yi_g[rows])
            s_ref[g] = _dot_x3(s, pt) + qt
        y_groups.append(jnp.concatenate(y_rows, axis=0))
    y = jnp.concatenate(y_groups, axis=1)

    inv_n = 1.0 / A_HEAD_DIM
    mean = _group_sum(y, ones_bd) * inv_n
    d = y - mean
    var = _group_sum(d * d, ones_bd) * inv_n
    yn = d * lax.rsqrt(var + LNX_EPS) * lnw_ref[...] + lnb_ref[...]
    out_ref[0] = ((yn + bonus) * gate).astype(out_ref.dtype)


def _rwkv(za, mu, w0, a0, w_lora, w_gate, k_k, k_a, r_k, ln_w, ln_b, ones_bd):
    bsz, t, _ = za.shape
    tc = min(SCAN_T, t)
    halo_rows = 8
    per = tc // halo_rows
    vec = lambda a: a.reshape(1, -1).astype(f32)
    return pl.pallas_call(
        _rwkv_body,
        grid=(bsz, t // tc),
        in_specs=[
            pl.BlockSpec((1, tc, A_IN), lambda b, i: (b, i, 0)),
            pl.BlockSpec((1, halo_rows, A_IN), lambda b, i: (b, jnp.maximum(i * per - 1, 0), 0)),
            _const_spec((1, A_IN)),
            _const_spec((1, A_WIDTH)),
            _const_spec((1, A_WIDTH)),
            _const_spec((LANES, 2 * A_WIDTH)),
            _const_spec((GATE_LORA, A_WIDTH)),
            _const_spec((1, A_WIDTH)),
            _const_spec((1, A_WIDTH)),
            _const_spec((1, A_WIDTH)),
            _const_spec((1, A_WIDTH)),
            _const_spec((1, A_WIDTH)),
            _const_spec((A_WIDTH, A_WIDTH)),
        ],
        out_specs=pl.BlockSpec((1, tc, A_WIDTH), lambda b, i: (b, i, 0)),
        out_shape=jax.ShapeDtypeStruct((bsz, t, A_WIDTH), bf16),
        scratch_shapes=[pltpu.VMEM((A_WIDTH // MXU_DIM, MXU_DIM, MXU_DIM), f32)],
        compiler_params=pltpu.CompilerParams(
            dimension_semantics=("parallel", "arbitrary"), vmem_limit_bytes=VMEM_LIMIT),
        name="rwkv7",
    )(za, za, vec(mu), vec(w0), vec(a0), w_lora, w_gate, vec(k_k), vec(k_a), vec(r_k),
      vec(ln_w), vec(ln_b), ones_bd)


def _attn_body(slope_ref, q_ref, k_ref, v_ref, diag_ref, lam_ref, sub_ref, o_ref, m_ref, l_ref, acc_ref,
               *, one_minus_lambda_init, lambda_init):
    hd = pl.program_id(1)
    qb = pl.program_id(2)
    bq = q_ref.shape[0]
    slope = slope_ref[hd]

    lane = lax.broadcasted_iota(jnp.int32, (1, 2 * B_HEAD_DIM), 1)
    q = q_ref[...]
    zero = jnp.zeros_like(q)
    qs = jnp.concatenate([jnp.where(lane < B_HEAD_DIM, q, zero),
                          jnp.where(lane >= B_HEAD_DIM, q, zero)], axis=0)

    m_ref[...] = jnp.full_like(m_ref, NEG_BIG)
    l_ref[...] = jnp.zeros_like(l_ref)
    acc_ref[...] = jnp.zeros_like(acc_ref)

    def update(s, v_blk):
        m_old = m_ref[...]
        m_new = jnp.maximum(m_old, jnp.max(s, axis=-1, keepdims=True))
        alpha = jnp.exp(m_old - m_new)
        p = jnp.exp(s - m_new)
        l_ref[...] = alpha * l_ref[...] + jnp.sum(p, axis=-1, keepdims=True)
        acc_ref[...] = alpha * acc_ref[...] + _dot(p.astype(bf16), v_blk)
        m_ref[...] = m_new

    kpos = lax.broadcasted_iota(jnp.int32, (1, bq), 1).astype(f32)

    def off_diag(j, carry):
        start = pl.multiple_of(j * bq, bq)
        k_blk = k_ref[pl.ds(start, bq), :]
        v_blk = v_ref[pl.ds(start, bq), :]
        bias = slope * (kpos + (j * bq).astype(f32))
        update(_dot_nt(qs, k_blk) + bias, v_blk)
        return carry

    lax.fori_loop(0, qb, off_diag, 0)

    start = pl.multiple_of(qb * bq, bq)
    k_blk = k_ref[pl.ds(start, bq), :]
    v_blk = v_ref[pl.ds(start, bq), :]
    dbias = diag_ref[0] + slope * (qb * bq).astype(f32)
    s = _dot_nt(qs, k_blk) + jnp.concatenate([dbias, dbias], axis=0)
    update(s, v_blk)

    lq1, lk1, lq2, lk2 = lam_ref[0:1, :], lam_ref[1:2, :], lam_ref[2:3, :], lam_ref[3:4, :]
    lam = (jnp.exp(jnp.sum(lq1 * lk1, axis=-1, keepdims=True))
           - jnp.exp(jnp.sum(lq2 * lk2, axis=-1, keepdims=True)) + lambda_init)
    o = acc_ref[...] / l_ref[...]
    o = o[:bq] - lam * o[bq:]
    ms = jnp.mean(o * o, axis=-1, keepdims=True)
    o = o * lax.rsqrt(ms + NORM_EPS) * sub_ref[...] * one_minus_lambda_init
    o_ref[...] = o.astype(o_ref.dtype)


def _attention(q, k, v, bsz, t, lam_vecs, subln_w, lambda_init):
    bq = min(ATT_BQ, t)
    nq = t // bq
    slopes = jnp.exp2(-8.0 * jnp.arange(1, B_HEADS + 1, dtype=f32) / B_HEADS)
    qq = jnp.arange(bq)[:, None]
    kk = jnp.arange(bq)[None, :]
    rel = jnp.where(kk <= qq, kk, 2 * qq - kk).astype(f32)
    allowed = (kk // CHUNK) <= (qq // CHUNK)
    diag = jnp.where(allowed[None], slopes[:, None, None] * rel[None], NEG_BIG)
    hw = 2 * B_HEAD_DIM
    body = functools.partial(_attn_body, one_minus_lambda_init=1.0 - lambda_init, lambda_init=lambda_init)
    return pl.pallas_call(
        body,
        grid=(bsz, B_HEADS, nq),
        in_specs=[
            pl.BlockSpec(memory_space=pltpu.SMEM),
            pl.BlockSpec((bq, hw), lambda b, h, i: (b * nq + i, h)),
            pl.BlockSpec((t, hw), lambda b, h, i: (b, h)),
            pl.BlockSpec((t, hw), lambda b, h, i: (b, h)),
            pl.BlockSpec((1, bq, bq), lambda b, h, i: (h, 0, 0)),
            pl.BlockSpec((4, B_HEAD_DIM), lambda b, h, i: (0, 0)),
            pl.BlockSpec((1, hw), lambda b, h, i: (0, 0)),
        ],
        out_specs=pl.BlockSpec((bq, hw), lambda b, h, i: (b * nq + i, h)),
        out_shape=jax.ShapeDtypeStruct((bsz * t, B_WIDTH), bf16),
        scratch_shapes=[
            pltpu.VMEM((2 * bq, 1), f32),
            pltpu.VMEM((2 * bq, 1), f32),
            pltpu.VMEM((2 * bq, hw), f32),
        ],
        compiler_params=pltpu.CompilerParams(
            dimension_semantics=("parallel", "parallel", "arbitrary"), vmem_limit_bytes=VMEM_LIMIT),
        name="diff_attn",
    )(slopes, q, k, v, diag, lam_vecs, subln_w.reshape(1, hw).astype(f32))


def _ffn_body(x_ref, ya_ref, yb_ref, wo_ref, nw_ref, wup_ref, cw_ref, cb_ref, wdn_ref, o_ref,
              carry_ref, acc_ref, h_ref):
    tb = pl.program_id(1)
    tm = x_ref.shape[1]
    x2 = x_ref[0] + _dot(ya_ref[0], wo_ref[:A_WIDTH, :]) + _dot(yb_ref[0], wo_ref[A_WIDTH:, :])
    acc_ref[...] = x2
    ms = jnp.mean(x2 * x2, axis=-1, keepdims=True)
    h_ref[...] = (x2 * lax.rsqrt(ms + NORM_EPS) * nw_ref[...]).astype(bf16)

    @pl.when(tb == 0)
    def _():
        carry_ref[...] = jnp.zeros_like(carry_ref)

    def conv(u, col):
        old = carry_ref[:, pl.ds(col, FFN_TF)]
        carry_ref[:, pl.ds(col, FFN_TF)] = u[tm - 8:, :]
        ext = jnp.concatenate([old, u], axis=0)
        cw = cw_ref[:, pl.ds(col, FFN_TF)]
        return (cw[0:1, :] * ext[6:6 + tm] + cw[1:2, :] * ext[7:7 + tm] + cw[2:3, :] * u
                + cb_ref[:, pl.ds(col, FFN_TF)])

    def chunk(j, carry):
        gcol = pl.multiple_of(j * FFN_TF, FFN_TF)
        ucol = pl.multiple_of(D_FF + j * FFN_TF, LANES)
        hb = h_ref[...]
        gate = conv(_dot(hb, wup_ref[:, pl.ds(gcol, FFN_TF)]), gcol)
        up = conv(_dot(hb, wup_ref[:, pl.ds(ucol, FFN_TF)]), ucol)
        act = (gate * jax.nn.sigmoid(gate) * up).astype(bf16)
        acc_ref[...] += _dot(act, wdn_ref[pl.ds(gcol, FFN_TF), :])
        return carry

    lax.fori_loop(0, D_FF // FFN_TF, chunk, 0)
    o_ref[0] = acc_ref[...]


def _out_ffn(x, ya, yb, w_out_bf, norm_w, w_up_bf, conv_w, conv_b, w_down_bf):
    bsz, t, _ = x.shape
    tm = min(FFN_TM, t)
    tok = lambda b, i: (b, i, 0)
    return pl.pallas_call(
        _ffn_body,
        grid=(bsz, t // tm),
        in_specs=[
            pl.BlockSpec((1, tm, D_MODEL), tok),
            pl.BlockSpec((1, tm, A_WIDTH), tok),
            pl.BlockSpec((1, tm, B_WIDTH), tok),
            _const_spec((A_WIDTH + B_WIDTH, D_MODEL)),
            _const_spec((1, D_MODEL)),
            _const_spec((D_MODEL, 2 * D_FF)),
            _const_spec((3, 2 * D_FF)),
            _const_spec((1, 2 * D_FF)),
            _const_spec((D_FF, D_MODEL)),
        ],
        out_specs=pl.BlockSpec((1, tm, D_MODEL), tok),
        out_shape=jax.ShapeDtypeStruct((bsz, t, D_MODEL), f32),
        scratch_shapes=[
            pltpu.VMEM((8, 2 * D_FF), f32),
            pltpu.VMEM((tm, D_MODEL), f32),
            pltpu.VMEM((tm, D_MODEL), bf16),
        ],
        compiler_params=pltpu.CompilerParams(
            dimension_semantics=("parallel", "arbitrary"), vmem_limit_bytes=VMEM_LIMIT),
        name="out_ffn",
    )(x, ya, yb, w_out_bf, norm_w, w_up_bf, conv_w, conv_b, w_down_bf)


def _layer(x, lambda_init, attn_norm_w, w_in, mu_shift, w0, w_decay_up, a0, w_aaa_up, w_gate_up,
           k_k, k_a, r_k, ln_x_w, ln_x_b, q_norm_w, k_norm_w, lambda_q1, lambda_k1, lambda_q2,
           lambda_k2, subln_w, w_out, ffn_norm_w, w_ffn_up, ffn_conv_w, ffn_conv_b, w_ffn_down):
    bsz, t, _ = x.shape
    head_ids = jnp.arange(A_WIDTH) // A_HEAD_DIM
    ones_bd = (head_ids[:, None] == head_ids[None, :]).astype(bf16)
    scale = 1.0 / math.sqrt(B_HEAD_DIM)
    qk_w = jnp.stack([jnp.tile(q_norm_w.astype(f32), B_WIDTH // B_HEAD_DIM) * scale,
                      jnp.tile(k_norm_w.astype(f32), B_WIDTH // B_HEAD_DIM)])
    za, q, k, v = _inproj(x.reshape(bsz * t, D_MODEL), attn_norm_w.reshape(1, D_MODEL),
                          w_in.astype(bf16), qk_w, ones_bd)

    zeros = jnp.zeros((DECAY_LORA, A_WIDTH), f32)
    w_lora = jnp.concatenate([jnp.concatenate([w_decay_up, zeros], axis=1),
                              jnp.concatenate([zeros, w_aaa_up], axis=1)], axis=0).astype(bf16)
    ya = _rwkv(za.reshape(bsz, t, A_IN), mu_shift, w0, a0, w_lora, w_gate_up.astype(bf16),
               k_k, k_a, r_k, ln_x_w, ln_x_b, ones_bd)

    lam_vecs = jnp.stack([lambda_q1, lambda_k1, lambda_q2, lambda_k2]).astype(f32)
    yb = _attention(q, k, v, bsz, t, lam_vecs, subln_w, lambda_init)

    return _out_ffn(x, ya, yb.reshape(bsz, t, B_WIDTH), w_out.astype(bf16),
                    ffn_norm_w.reshape(1, D_MODEL), w_ffn_up.astype(bf16), ffn_conv_w,
                    ffn_conv_b.reshape(1, 2 * D_FF), w_ffn_down.astype(bf16))


def kernel(x, attn_norm_w, w_in, mu_shift, w0, w_decay_up, a0, w_aaa_up, w_gate_up, k_k, k_a, r_k,
           ln_x_w, ln_x_b, q_norm_w, k_norm_w, lambda_q1, lambda_k1, lambda_q2, lambda_k2, subln_w,
           w_out, ffn_norm_w, w_ffn_up, ffn_conv_w, ffn_conv_b, w_ffn_down):
    params = (attn_norm_w, w_in, mu_shift, w0, w_decay_up, a0, w_aaa_up, w_gate_up, k_k, k_a, r_k,
              ln_x_w, ln_x_b, q_norm_w, k_norm_w, lambda_q1, lambda_k1, lambda_q2, lambda_k2,
              subln_w, w_out, ffn_norm_w, w_ffn_up, ffn_conv_w, ffn_conv_b, w_ffn_down)
    for l in range(attn_norm_w.shape[0]):
        lambda_init = 0.8 - 0.6 * math.exp(-0.3 * l)
        x = _layer(x, lambda_init, *(p[l] for p in params))
    return x
```

```python
import functools
import math

import jax
import jax.numpy as jnp
from jax import lax
from jax.experimental import pallas as pl
from jax.experimental.pallas import tpu as pltpu

f32 = jnp.float32
bf16 = jnp.bfloat16

D_MODEL = 1024
CHUNK = 64
A_HEADS = 8
A_HEAD_DIM = 64
A_WIDTH = A_HEADS * A_HEAD_DIM
DECAY_LORA = 64
AAA_LORA = 64
GATE_LORA = 128
A_IN = 3 * A_WIDTH + DECAY_LORA + AAA_LORA + GATE_LORA
B_HEADS = 4
B_HEAD_DIM = 64
B_WIDTH = B_HEADS * 2 * B_HEAD_DIM
B_IN = 3 * B_WIDTH
IN_WIDTH = A_IN + B_IN
D_FF = 2816
NORM_EPS = 1e-6
LNX_EPS = 64e-5
DECAY_SCALE = math.exp(-0.5)
L2_EPS = 1e-12

LANES = 128
MXU_DIM = 256
VMEM_LIMIT = 56 * 1024 * 1024

IN_TM = 512
SCAN_T = 256
ATT_BQ = 512
ATT_SUB = 256
ATT_ONES_ROWS = 16
LOG2_E = math.log2(math.e)
FFN_TM = 512
FFN_TF = 256
NEG_BIG = -1e30


def _dot(a, b, dims=((1,), (0,))):
    return lax.dot_general(a, b, (dims, ((), ())), preferred_element_type=f32)


def _dot_nt(a, b):
    return _dot(a, b, ((1,), (1,)))


def _split2(x):
    hi = x.astype(bf16)
    lo = (x - hi.astype(f32)).astype(bf16)
    return hi, lo


def _split3(x):
    hi = x.astype(bf16)
    r1 = x - hi.astype(f32)
    mid = r1.astype(bf16)
    lo = (r1 - mid.astype(f32)).astype(bf16)
    return hi, mid, lo


def _dot_x3(a, b, dims=((1,), (0,))):
    ah, al = _split2(a)
    bh, bl = _split2(b)
    return _dot(ah, bh, dims) + (_dot(ah, bl, dims) + _dot(al, bh, dims))


def _group_sum(x, ones_bd):
    hi, lo = _split2(x)
    return _dot(hi, ones_bd) + _dot(lo, ones_bd)


def _const_spec(shape):
    nd = len(shape)
    return pl.BlockSpec(shape, lambda *_: (0,) * nd, pipeline_mode=pl.Buffered(1))


def _inproj_body(x_ref, nw_ref, w_ref, qkw_ref, ones_ref, za_ref, q_ref, k_ref, v_ref):
    x = x_ref[...]
    ms = jnp.mean(x * x, axis=-1, keepdims=True)
    h = (x * lax.rsqrt(ms + NORM_EPS) * nw_ref[...]).astype(bf16)
    za_ref[...] = _dot(h, w_ref[:, :A_IN])
    ones_bd = ones_ref[...]

    def qk_norm(col0, row):
        z = _dot(h, w_ref[:, col0:col0 + B_WIDTH])
        ss = _group_sum(z * z, ones_bd)
        return (z * lax.rsqrt(ss * (1.0 / B_HEAD_DIM) + NORM_EPS) * qkw_ref[row:row + 1, :]).astype(bf16)

    q_ref[...] = qk_norm(A_IN, 0)
    k_ref[...] = qk_norm(A_IN + B_WIDTH, 1)
    v_ref[...] = _dot(h, w_ref[:, A_IN + 2 * B_WIDTH:]).astype(bf16)


def _inproj(x2d, norm_w, w_in_bf, qk_w, ones_bd):
    n = x2d.shape[0]
    tm = min(IN_TM, n)
    row = lambda i: (i, 0)
    return pl.pallas_call(
        _inproj_body,
        grid=(n // tm,),
        in_specs=[
            pl.BlockSpec((tm, D_MODEL), row),
            _const_spec((1, D_MODEL)),
            _const_spec((D_MODEL, IN_WIDTH)),
            _const_spec((2, B_WIDTH)),
            _const_spec((B_WIDTH, B_WIDTH)),
        ],
        out_specs=[
            pl.BlockSpec((tm, A_IN), row),
            pl.BlockSpec((tm, B_WIDTH), row),
            pl.BlockSpec((tm, B_WIDTH), row),
            pl.BlockSpec((tm, B_WIDTH), row),
        ],
        out_shape=[
            jax.ShapeDtypeStruct((n, A_IN), f32),
            jax.ShapeDtypeStruct((n, B_WIDTH), bf16),
            jax.ShapeDtypeStruct((n, B_WIDTH), bf16),
            jax.ShapeDtypeStruct((n, B_WIDTH), bf16),
        ],
        compiler_params=pltpu.CompilerParams(
            dimension_semantics=("parallel",), vmem_limit_bytes=VMEM_LIMIT),
        name="inproj",
    )(x2d, norm_w, w_in_bf, qk_w, ones_bd)


def _apply_unit_lower_inverse(l_mat, x0):
    x = x0 + _dot(l_mat.astype(bf16), x0.astype(bf16))
    p = l_mat
    for _ in range(5):
        pb = p.astype(bf16)
        p = _dot(pb, pb)
        x = x + _dot(p.astype(bf16), x.astype(bf16))
    return x


def _rwkv_body(za_ref, halo_ref, mu_ref, w0_ref, a0_ref, wlora_ref, wgate_ref, kk_ref, ka_ref,
               rk_ref, lnw_ref, lnb_ref, ones_ref, out_ref, s_ref):
    tb = pl.program_id(1)
    tc = za_ref.shape[1]
    n_chunks = tc // CHUNK

    @pl.when(tb == 0)
    def _():
        s_ref[...] = jnp.zeros_like(s_ref)

    h = za_ref[0]
    last = jnp.where(tb == 0, 0.0, halo_ref[0][7:8, :])
    row0 = lax.broadcasted_iota(jnp.int32, (tc, 1), 0) == 0
    prev = jnp.where(row0, last, pltpu.roll(h, 1, axis=0))
    hs = h + (prev - h) * mu_ref[...]

    r = hs[:, 0:A_WIDTH]
    k = hs[:, A_WIDTH:2 * A_WIDTH]
    v = hs[:, 2 * A_WIDTH:3 * A_WIDTH]
    lora_in = hs[:, 3 * A_WIDTH:3 * A_WIDTH + LANES]
    gate_in = hs[:, 3 * A_WIDTH + LANES:]
    lane = lax.broadcasted_iota(jnp.int32, (1, LANES), 1)
    lora_act = jnp.where(lane < DECAY_LORA, jnp.tanh(lora_in), lora_in).astype(bf16)
    lora = _dot(lora_act, wlora_ref[...])
    logw = -DECAY_SCALE * jax.nn.sigmoid(w0_ref[...] + lora[:, :A_WIDTH])
    lr = jax.nn.sigmoid(a0_ref[...] + lora[:, A_WIDTH:])
    gate = _dot(jax.nn.sigmoid(gate_in).astype(bf16), wgate_ref[...])

    ones_bd = ones_ref[...]
    kk = k * kk_ref[...]
    kk = kk * lax.rsqrt(_group_sum(kk * kk, ones_bd) + L2_EPS)
    kh = k * (1.0 + (lr - 1.0) * ka_ref[...])
    a_s = -kk
    b_s = kk * lr
    bonus = _group_sum(r * kh * rk_ref[...], ones_bd) * v

    ri = lax.broadcasted_iota(jnp.int32, (tc, tc), 0)
    ci = lax.broadcasted_iota(jnp.int32, (tc, tc), 1)
    in_chunk = (ri & (CHUNK - 1)).astype(jnp.uint32)
    dist = (ri - ci).astype(jnp.uint32)
    incl = dist <= in_chunk
    strict = (dist - 1) < in_chunk
    tri = jnp.where(incl, 1.0, 0.0).astype(bf16)
    w_hi, w_mid, w_lo = _split3(logw)
    cs = _dot(tri, w_hi) + (_dot(tri, w_mid) + _dot(tri, w_lo))
    cs3 = cs.reshape(n_chunks, CHUNK, A_WIDTH)
    cs_end = jnp.broadcast_to(cs3[:, CHUNK - 1:CHUNK, :], cs3.shape).reshape(tc, A_WIDTH)

    e_neg = jnp.exp(-cs)
    e_end = jnp.exp(cs_end - cs)
    a_t = a_s * jnp.exp(cs - logw)
    r_t = r * jnp.exp(cs)
    b_t = b_s * e_neg
    k_t = kh * e_neg
    b_h = b_s * e_end
    k_h = kh * e_end
    g_end = jnp.exp(cs_end)

    w_parts, u_parts, rp_parts, yi_parts = [], [], [], []
    for p in range(A_HEADS // 2):
        sl = slice(p * LANES, (p + 1) * LANES)
        bk = jnp.concatenate([b_t[:, sl], k_t[:, sl]], axis=0).astype(bf16)
        w_p = u_p = rp_p = yi_p = None
        for hh in range(2):
            hm = (lane >= hh * A_HEAD_DIM) & (lane < (hh + 1) * A_HEAD_DIM)
            a_m = jnp.where(hm, a_t[:, sl], 0.0)
            r_m = jnp.where(hm, r_t[:, sl], 0.0)
            v_m = jnp.where(hm, v[:, sl], 0.0)
            ar = jnp.concatenate([a_m, r_m], axis=0).astype(bf16)
            sc = _dot_nt(ar, bk)
            l_ab = jnp.where(strict, sc[:tc, :tc], 0.0)
            l_ak = jnp.where(strict, sc[:tc, tc:], 0.0)
            m_rb = jnp.where(incl, sc[tc:, :tc], 0.0)
            m_rk = jnp.where(incl, sc[tc:, tc:], 0.0)
            v_mb = v_m.astype(bf16)
            x0 = jnp.concatenate([a_m, _dot(l_ak.astype(bf16), v_mb)], axis=1)
            x = _apply_unit_lower_inverse(l_ab, x0)
            w_m = x[:, :LANES]
            u_m = x[:, LANES:]
            m_rb_b = m_rb.astype(bf16)
            rp_m = r_m + _dot(m_rb_b, w_m.astype(bf16))
            yi_m = _dot(m_rb_b, u_m.astype(bf16)) + _dot(m_rk.astype(bf16), v_mb)
            if hh == 0:
                w_p, u_p, rp_p, yi_p = w_m, u_m, rp_m, yi_m
            else:
                w_p, u_p, rp_p, yi_p = w_p + w_m, u_p + u_m, rp_p + rp_m, yi_p + yi_m
        w_parts.append(w_p)
        u_parts.append(u_p)
        rp_parts.append(rp_p)
        yi_parts.append(yi_p)

    qi = lax.broadcasted_iota(jnp.int32, (MXU_DIM, MXU_DIM), 0)
    qj = lax.broadcasted_iota(jnp.int32, (MXU_DIM, MXU_DIM), 1)
    same_head = (qi ^ qj) < A_HEAD_DIM
    eye = qi == qj
    n_groups = A_WIDTH // MXU_DIM
    y_groups = []
    for g in range(n_groups):
        gs = slice(g * MXU_DIM, (g + 1) * MXU_DIM)
        cat = lambda parts: jnp.concatenate(parts[2 * g:2 * g + 2], axis=1)
        w_g, u_g, rp_g, yi_g = cat(w_parts), cat(u_parts), cat(rp_parts), cat(yi_parts)
        y_rows = []
        for c in range(n_chunks):
            rows = slice(c * CHUNK, (c + 1) * CHUNK)
            bh_c = b_h[rows, gs]
            pt = jnp.where(same_head, _dot(w_g[rows].T.astype(bf16), bh_c.astype(bf16)), 0.0)
            pt = pt + jnp.where(eye, g_end[c * CHUNK:c * CHUNK + 1, gs], 0.0)
            uv = jnp.concatenate([u_g[rows], v[rows, gs]], axis=0)
            bkh = jnp.concatenate([bh_c, k_h[rows, gs]], axis=0)
            qt = jnp.where(same_head, _dot(uv.T.astype(bf16), bkh.astype(bf16)), 0.0)
            s = s_ref[g]
            y_rows.append(_dot_x3(rp_g[rows], s, ((1,), (1,))) + yi_g[rows])
            s_ref[g] = _dot_x3(s, pt) + qt
        y_groups.append(jnp.concatenate(y_rows, axis=0))
    y = jnp.concatenate(y_groups, axis=1)

    inv_n = 1.0 / A_HEAD_DIM
    mean = _group_sum(y, ones_bd) * inv_n
    d = y - mean
    var = _group_sum(d * d, ones_bd) * inv_n
    yn = d * lax.rsqrt(var + LNX_EPS) * lnw_ref[...] + lnb_ref[...]
    out_ref[0] = ((yn + bonus) * gate).astype(out_ref.dtype)


def _rwkv(za, mu, w0, a0, w_lora, w_gate, k_k, k_a, r_k, ln_w, ln_b, ones_bd):
    bsz, t, _ = za.shape
    tc = min(SCAN_T, t)
    halo_rows = 8
    per = tc // halo_rows
    vec = lambda a: a.reshape(1, -1).astype(f32)
    return pl.pallas_call(
        _rwkv_body,
        grid=(bsz, t // tc),
        in_specs=[
            pl.BlockSpec((1, tc, A_IN), lambda b, i: (b, i, 0)),
            pl.BlockSpec((1, halo_rows, A_IN), lambda b, i: (b, jnp.maximum(i * per - 1, 0), 0)),
            _const_spec((1, A_IN)),
            _const_spec((1, A_WIDTH)),
            _const_spec((1, A_WIDTH)),
            _const_spec((LANES, 2 * A_WIDTH)),
            _const_spec((GATE_LORA, A_WIDTH)),
            _const_spec((1, A_WIDTH)),
            _const_spec((1, A_WIDTH)),
            _const_spec((1, A_WIDTH)),
            _const_spec((1, A_WIDTH)),
            _const_spec((1, A_WIDTH)),
            _const_spec((A_WIDTH, A_WIDTH)),
        ],
        out_specs=pl.BlockSpec((1, tc, A_WIDTH), lambda b, i: (b, i, 0)),
        out_shape=jax.ShapeDtypeStruct((bsz, t, A_WIDTH), bf16),
        scratch_shapes=[pltpu.VMEM((A_WIDTH // MXU_DIM, MXU_DIM, MXU_DIM), f32)],
        compiler_params=pltpu.CompilerParams(
            dimension_semantics=("parallel", "arbitrary"), vmem_limit_bytes=VMEM_LIMIT),
        name="rwkv7",
    )(za, za, vec(mu), vec(w0), vec(a0), w_lora, w_gate, vec(k_k), vec(k_a), vec(r_k),
      vec(ln_w), vec(ln_b), ones_bd)


def _attn_body(bias_ref, q_ref, k_ref, v_ref, diag_ref, lam_ref, sub_ref, o_ref,
               ka_ref, vt_ref, st_ref, m_ref, acc_ref, *, one_minus_lambda_init, lambda_init):
    hd = pl.program_id(1)
    qb = pl.program_id(2)
    bq = q_ref.shape[0]
    t = k_ref.shape[0]
    hw = 2 * B_HEAD_DIM
    n_bias = bias_ref.shape[1]

    @pl.when(qb == 0)
    def _():
        pos = lax.broadcasted_iota(jnp.int32, (t, hw), 0)
        col = lax.broadcasted_iota(jnp.int32, (t, hw), 1)
        pos_cols = jnp.where(col < n_bias, jnp.where((col & 1) == 0, pos >> 8, pos & 255), 0)
        ka_ref[:, :hw] = k_ref[...]
        ka_ref[:, hw:] = pos_cols.astype(f32).astype(bf16)
        vt_ref[:hw, :] = v_ref[...].astype(f32).T.astype(bf16)
        ones_row = lax.broadcasted_iota(jnp.int32, (ATT_ONES_ROWS, t), 0) == 0
        vt_ref[hw:, :] = jnp.where(ones_row, 1.0, 0.0).astype(bf16)

    lane = lax.broadcasted_iota(jnp.int32, (1, hw), 1)
    q = q_ref[...]
    zero = jnp.zeros_like(q)
    q_pos = jnp.zeros((1, hw), f32)
    for i in range(n_bias):
        q_pos = jnp.where(lane == i, bias_ref[hd, i], q_pos)
    q_pos = jnp.broadcast_to(q_pos, (bq, hw)).astype(bf16)
    qs = jnp.concatenate(
        [jnp.concatenate([jnp.where(lane < B_HEAD_DIM, q, zero), q_pos], axis=1),
         jnp.concatenate([jnp.where(lane >= B_HEAD_DIM, q, zero), q_pos], axis=1)], axis=0)

    m_ref[...] = jnp.full_like(m_ref, NEG_BIG)
    acc_ref[...] = jnp.zeros_like(acc_ref)

    def scores(j, slot):
        st_ref[slot] = _dot_nt(ka_ref[pl.ds(pl.multiple_of(j * bq, bq), bq), :], qs)

    def update(start, nkeys, st):
        m_old = m_ref[...]
        m_new = jnp.maximum(m_old, jnp.max(st, axis=0, keepdims=True))
        alpha = jnp.exp2(m_old - m_new)
        p = jnp.exp2(st - m_new).astype(bf16)
        acc_ref[...] = alpha * acc_ref[...] + _dot(vt_ref[:, pl.ds(start, nkeys)], p)
        m_ref[...] = m_new

    def off_diag(j, slot):
        scores(j + 1, 1 - slot)
        update(pl.multiple_of(j * bq, bq), bq, st_ref[slot])

    def off_diag_pair(jj, carry):
        off_diag(2 * jj, 0)
        off_diag(2 * jj + 1, 1)
        return carry

    scores(0, 0)
    lax.fori_loop(0, qb // 2, off_diag_pair, 0)

    @pl.when(qb % 2 == 1)
    def _():
        off_diag(qb - 1, 0)

    sub = diag_ref.shape[2]
    for d in range(bq // sub):
        st = st_ref[qb & 1, d * sub:(d + 1) * sub, :] + diag_ref[0, d]
        update(pl.multiple_of(qb * bq + d * sub, sub), sub, st)

    lq1, lk1, lq2, lk2 = lam_ref[0:1, :], lam_ref[1:2, :], lam_ref[2:3, :], lam_ref[3:4, :]
    lam = (jnp.exp(jnp.sum(lq1 * lk1, axis=-1, keepdims=True))
           - jnp.exp(jnp.sum(lq2 * lk2, axis=-1, keepdims=True)) + lambda_init)
    ot = acc_ref[:hw, :] / acc_ref[hw:hw + 1, :]
    o = (ot[:, :bq] - lam * ot[:, bq:]).T
    ms = jnp.mean(o * o, axis=-1, keepdims=True)
    o = o * lax.rsqrt(ms + NORM_EPS) * sub_ref[...] * one_minus_lambda_init
    o_ref[...] = o.astype(o_ref.dtype)


def _attention(q, k, v, bsz, t, lam_vecs, subln_w, lambda_init):
    bq = min(ATT_BQ, t)
    nq = t // bq
    assert t <= 256 * 256, "key positions are split into two bf16-exact columns"
    slopes = jnp.exp2(-8.0 * jnp.arange(1, B_HEADS + 1, dtype=f32) / B_HEADS) * LOG2_E
    c_hi, c_mid, c_lo = (piece.astype(f32) for piece in _split3(slopes))
    bias_cols = jnp.stack([256.0 * c_hi, c_hi, 256.0 * c_mid, c_mid, 256.0 * c_lo, c_lo], axis=1)
    sub = min(ATT_SUB, bq)
    kk = jnp.arange(bq)[:, None]
    qq = jnp.arange(bq)[None, :]
    later = jnp.where(kk > qq, 2.0 * (qq - kk), 0.0).astype(f32)
    allowed = (kk // CHUNK) <= (qq // CHUNK)
    diag = jnp.where(allowed[None], slopes[:, None, None] * later[None], NEG_BIG)
    diag = jnp.concatenate([diag, diag], axis=2)
    diag = diag.reshape(B_HEADS, bq // sub, sub, 2 * bq)
    hw = 2 * B_HEAD_DIM
    body = functools.partial(_attn_body, one_minus_lambda_init=1.0 - lambda_init, lambda_init=lambda_init)
    return pl.pallas_call(
        body,
        grid=(bsz, B_HEADS, nq),
        in_specs=[
            pl.BlockSpec(memory_space=pltpu.SMEM),
            pl.BlockSpec((bq, hw), lambda b, h, i: (b * nq + i, h)),
            pl.BlockSpec((t, hw), lambda b, h, i: (b, h)),
            pl.BlockSpec((t, hw), lambda b, h, i: (b, h)),
            pl.BlockSpec((1, bq // sub, sub, 2 * bq), lambda b, h, i: (h, 0, 0, 0)),
            pl.BlockSpec((4, B_HEAD_DIM), lambda b, h, i: (0, 0)),
            pl.BlockSpec((1, hw), lambda b, h, i: (0, 0)),
        ],
        out_specs=pl.BlockSpec((bq, hw), lambda b, h, i: (b * nq + i, h)),
        out_shape=jax.ShapeDtypeStruct((bsz * t, B_WIDTH), bf16),
        scratch_shapes=[
            pltpu.VMEM((t, 2 * hw), bf16),
            pltpu.VMEM((hw + ATT_ONES_ROWS, t), bf16),
            pltpu.VMEM((2, bq, 2 * bq), f32),
            pltpu.VMEM((1, 2 * bq), f32),
            pltpu.VMEM((hw + ATT_ONES_ROWS, 2 * bq), f32),
        ],
        compiler_params=pltpu.CompilerParams(
            dimension_semantics=("parallel", "parallel", "arbitrary"), vmem_limit_bytes=VMEM_LIMIT),
        name="diff_attn",
    )(bias_cols, q, k, v, diag, lam_vecs, subln_w.reshape(1, hw).astype(f32))


def _ffn_body(x_ref, ya_ref, yb_ref, wo_ref, nw_ref, wup_ref, cw_ref, cb_ref, wdn_ref, o_ref,
              carry_ref, acc_ref, h_ref):
    tb = pl.program_id(1)
    tm = x_ref.shape[1]
    x2 = x_ref[0] + _dot(ya_ref[0], wo_ref[:A_WIDTH, :]) + _dot(yb_ref[0], wo_ref[A_WIDTH:, :])
    acc_ref[...] = x2
    ms = jnp.mean(x2 * x2, axis=-1, keepdims=True)
    h_ref[...] = (x2 * lax.rsqrt(ms + NORM_EPS) * nw_ref[...]).astype(bf16)

    @pl.when(tb == 0)
    def _():
        carry_ref[...] = jnp.zeros_like(carry_ref)

    def conv(u, col):
        old = carry_ref[:, pl.ds(col, FFN_TF)]
        carry_ref[:, pl.ds(col, FFN_TF)] = u[tm - 8:, :]
        ext = jnp.concatenate([old, u], axis=0)
        cw = cw_ref[:, pl.ds(col, FFN_TF)]
        return (cw[0:1, :] * ext[6:6 + tm] + cw[1:2, :] * ext[7:7 + tm] + cw[2:3, :] * u
                + cb_ref[:, pl.ds(col, FFN_TF)])

    def chunk(j, carry):
        gcol = pl.multiple_of(j * FFN_TF, FFN_TF)
        ucol = pl.multiple_of(D_FF + j * FFN_TF, LANES)
        hb = h_ref[...]
        gate = conv(_dot(hb, wup_ref[:, pl.ds(gcol, FFN_TF)]), gcol)
        up = conv(_dot(hb, wup_ref[:, pl.ds(ucol, FFN_TF)]), ucol)
        act = (gate * jax.nn.sigmoid(gate) * up).astype(bf16)
        acc_ref[...] += _dot(act, wdn_ref[pl.ds(gcol, FFN_TF), :])
        return carry

    lax.fori_loop(0, D_FF // FFN_TF, chunk, 0)
    o_ref[0] = acc_ref[...]


def _out_ffn(x, ya, yb, w_out_bf, norm_w, w_up_bf, conv_w, conv_b, w_down_bf):
    bsz, t, _ = x.shape
    tm = min(FFN_TM, t)
    tok = lambda b, i: (b, i, 0)
    return pl.pallas_call(
        _ffn_body,
        grid=(bsz, t // tm),
        in_specs=[
            pl.BlockSpec((1, tm, D_MODEL), tok),
            pl.BlockSpec((1, tm, A_WIDTH), tok),
            pl.BlockSpec((1, tm, B_WIDTH), tok),
            _const_spec((A_WIDTH + B_WIDTH, D_MODEL)),
            _const_spec((1, D_MODEL)),
            _const_spec((D_MODEL, 2 * D_FF)),
            _const_spec((3, 2 * D_FF)),
            _const_spec((1, 2 * D_FF)),
            _const_spec((D_FF, D_MODEL)),
        ],
        out_specs=pl.BlockSpec((1, tm, D_MODEL), tok),
        out_shape=jax.ShapeDtypeStruct((bsz, t, D_MODEL), f32),
        scratch_shapes=[
            pltpu.VMEM((8, 2 * D_FF), f32),
            pltpu.VMEM((tm, D_MODEL), f32),
            pltpu.VMEM((tm, D_MODEL), bf16),
        ],
        compiler_params=pltpu.CompilerParams(
            dimension_semantics=("parallel", "arbitrary"), vmem_limit_bytes=VMEM_LIMIT),
        name="out_ffn",
    )(x, ya, yb, w_out_bf, norm_w, w_up_bf, conv_w, conv_b, w_down_bf)


def _layer(x, lambda_init, attn_norm_w, w_in, mu_shift, w0, w_decay_up, a0, w_aaa_up, w_gate_up,
           k_k, k_a, r_k, ln_x_w, ln_x_b, q_norm_w, k_norm_w, lambda_q1, lambda_k1, lambda_q2,
           lambda_k2, subln_w, w_out, ffn_norm_w, w_ffn_up, ffn_conv_w, ffn_conv_b, w_ffn_down):
    bsz, t, _ = x.shape
    head_ids = jnp.arange(A_WIDTH) // A_HEAD_DIM
    ones_bd = (head_ids[:, None] == head_ids[None, :]).astype(bf16)
    scale = LOG2_E / math.sqrt(B_HEAD_DIM)
    qk_w = jnp.stack([jnp.tile(q_norm_w.astype(f32), B_WIDTH // B_HEAD_DIM) * scale,
                      jnp.tile(k_norm_w.astype(f32), B_WIDTH // B_HEAD_DIM)])
    za, q, k, v = _inproj(x.reshape(bsz * t, D_MODEL), attn_norm_w.reshape(1, D_MODEL),
                          w_in.astype(bf16), qk_w, ones_bd)

    zeros = jnp.zeros((DECAY_LORA, A_WIDTH), f32)
    w_lora = jnp.concatenate([jnp.concatenate([w_decay_up, zeros], axis=1),
                              jnp.concatenate([zeros, w_aaa_up], axis=1)], axis=0).astype(bf16)
    ya = _rwkv(za.reshape(bsz, t, A_IN), mu_shift, w0, a0, w_lora, w_gate_up.astype(bf16),
               k_k, k_a, r_k, ln_x_w, ln_x_b, ones_bd)

    lam_vecs = jnp.stack([lambda_q1, lambda_k1, lambda_q2, lambda_k2]).astype(f32)
    yb = _attention(q, k, v, bsz, t, lam_vecs, subln_w, lambda_init)

    return _out_ffn(x, ya, yb.reshape(bsz, t, B_WIDTH), w_out.astype(bf16),
                    ffn_norm_w.reshape(1, D_MODEL), w_ffn_up.astype(bf16), ffn_conv_w,
                    ffn_conv_b.reshape(1, 2 * D_FF), w_ffn_down.astype(bf16))


def kernel(x, attn_norm_w, w_in, mu_shift, w0, w_decay_up, a0, w_aaa_up, w_gate_up, k_k, k_a, r_k,
           ln_x_w, ln_x_b, q_norm_w, k_norm_w, lambda_q1, lambda_k1, lambda_q2, lambda_k2, subln_w,
           w_out, ffn_norm_w, w_ffn_up, ffn_conv_w, ffn_conv_b, w_ffn_down):
    params = (attn_norm_w, w_in, mu_shift, w0, w_decay_up, a0, w_aaa_up, w_gate_up, k_k, k_a, r_k,
              ln_x_w, ln_x_b, q_norm_w, k_norm_w, lambda_q1, lambda_k1, lambda_q2, lambda_k2,
              subln_w, w_out, ffn_norm_w, w_ffn_up, ffn_conv_w, ffn_conv_b, w_ffn_down)
    for l in range(attn_norm_w.shape[0]):
        lambda_init = 0.8 - 0.6 * math.exp(-0.3 * l)
        x = _layer(x, lambda_init, *(p[l] for p in params))
    return x
```

```python
import functools
import math

import jax
import jax.numpy as jnp
from jax import lax
from jax.experimental import pallas as pl
from jax.experimental.pallas import tpu as pltpu

f32 = jnp.float32
bf16 = jnp.bfloat16

D_MODEL = 1024
CHUNK = 64
A_HEADS = 8
A_HEAD_DIM = 64
A_WIDTH = A_HEADS * A_HEAD_DIM
DECAY_LORA = 64
AAA_LORA = 64
GATE_LORA = 128
A_IN = 3 * A_WIDTH + DECAY_LORA + AAA_LORA + GATE_LORA
B_HEADS = 4
B_HEAD_DIM = 64
B_WIDTH = B_HEADS * 2 * B_HEAD_DIM
B_IN = 3 * B_WIDTH
IN_WIDTH = A_IN + B_IN
D_FF = 2816
NORM_EPS = 1e-6
LNX_EPS = 64e-5
DECAY_SCALE = math.exp(-0.5)
L2_EPS = 1e-12

LANES = 128
MXU_DIM = 256
VMEM_LIMIT = 56 * 1024 * 1024

IN_TM = 512
SCAN_T = 256
ATT_BQ = 512
ATT_SUB = 512
ATT_ONES_ROWS = 16
LOG2_E = math.log2(math.e)
FFN_TM = 512
FFN_TF = 256
NEG_BIG = -1e30


def _dot(a, b, dims=((1,), (0,))):
    return lax.dot_general(a, b, (dims, ((), ())), preferred_element_type=f32)


def _dot_nt(a, b):
    return _dot(a, b, ((1,), (1,)))


def _split2(x):
    hi = x.astype(bf16)
    lo = (x - hi.astype(f32)).astype(bf16)
    return hi, lo


def _split3(x):
    hi = x.astype(bf16)
    r1 = x - hi.astype(f32)
    mid = r1.astype(bf16)
    lo = (r1 - mid.astype(f32)).astype(bf16)
    return hi, mid, lo


def _group_sum(x, ones_bd):
    hi, lo = _split2(x)
    parts = []
    for g in range(x.shape[1] // MXU_DIM):
        cols = slice(g * MXU_DIM, (g + 1) * MXU_DIM)
        parts.append(_dot(hi[:, cols], ones_bd) + _dot(lo[:, cols], ones_bd))
    return jnp.concatenate(parts, axis=1)


def _const_spec(shape):
    nd = len(shape)
    return pl.BlockSpec(shape, lambda *_: (0,) * nd, pipeline_mode=pl.Buffered(1))


def _inproj_body(x_ref, nw_ref, w_ref, qkw_ref, ones_ref, za_ref, q_ref, k_ref, v_ref):
    x = x_ref[...]
    ms = jnp.mean(x * x, axis=-1, keepdims=True)
    h = (x * lax.rsqrt(ms + NORM_EPS) * nw_ref[...]).astype(bf16)
    za_ref[...] = _dot(h, w_ref[:, :A_IN])
    ones_bd = ones_ref[...]

    def qk_norm(col0, row):
        z = _dot(h, w_ref[:, col0:col0 + B_WIDTH])
        ss = _group_sum(z * z, ones_bd)
        return (z * lax.rsqrt(ss * (1.0 / B_HEAD_DIM) + NORM_EPS) * qkw_ref[row:row + 1, :]).astype(bf16)

    q_ref[...] = qk_norm(A_IN, 0)
    k_ref[...] = qk_norm(A_IN + B_WIDTH, 1)
    v_ref[...] = _dot(h, w_ref[:, A_IN + 2 * B_WIDTH:]).astype(bf16)


def _inproj(x2d, norm_w, w_in_bf, qk_w, ones_bd):
    n = x2d.shape[0]
    tm = min(IN_TM, n)
    row = lambda i: (i, 0)
    return pl.pallas_call(
        _inproj_body,
        grid=(n // tm,),
        in_specs=[
            pl.BlockSpec((tm, D_MODEL), row),
            _const_spec((1, D_MODEL)),
            _const_spec((D_MODEL, IN_WIDTH)),
            _const_spec((2, B_WIDTH)),
            _const_spec((MXU_DIM, MXU_DIM)),
        ],
        out_specs=[
            pl.BlockSpec((tm, A_IN), row),
            pl.BlockSpec((tm, B_WIDTH), row),
            pl.BlockSpec((tm, B_WIDTH), row),
            pl.BlockSpec((tm, B_WIDTH), row),
        ],
        out_shape=[
            jax.ShapeDtypeStruct((n, A_IN), f32),
            jax.ShapeDtypeStruct((n, B_WIDTH), bf16),
            jax.ShapeDtypeStruct((n, B_WIDTH), bf16),
            jax.ShapeDtypeStruct((n, B_WIDTH), bf16),
        ],
        compiler_params=pltpu.CompilerParams(
            dimension_semantics=("parallel",), vmem_limit_bytes=VMEM_LIMIT),
        name="inproj",
    )(x2d, norm_w, w_in_bf, qk_w, ones_bd)


def _rwkv_body(za_ref, halo_ref, mu_ref, w0_ref, a0_ref, wlora_ref, wgate_ref, kk_ref, ka_ref,
               rk_ref, lnw_ref, lnb_ref, ones_ref, out_ref, s_ref):
    tb = pl.program_id(1)
    tc = za_ref.shape[1]
    n_chunks = tc // CHUNK

    @pl.when(tb == 0)
    def _():
        s_ref[...] = jnp.zeros_like(s_ref)

    h = za_ref[0]
    last = jnp.where(tb == 0, 0.0, halo_ref[0][7:8, :])
    row0 = lax.broadcasted_iota(jnp.int32, (tc, 1), 0) == 0
    prev = jnp.where(row0, last, pltpu.roll(h, 1, axis=0))
    hs = h + (prev - h) * mu_ref[...]

    r = hs[:, 0:A_WIDTH]
    k = hs[:, A_WIDTH:2 * A_WIDTH]
    v = hs[:, 2 * A_WIDTH:3 * A_WIDTH]
    lora_in = hs[:, 3 * A_WIDTH:3 * A_WIDTH + LANES]
    gate_in = hs[:, 3 * A_WIDTH + LANES:]
    lane = lax.broadcasted_iota(jnp.int32, (1, LANES), 1)
    lora_act = jnp.where(lane < DECAY_LORA, jnp.tanh(lora_in), lora_in).astype(bf16)
    lora = _dot(lora_act, wlora_ref[...])
    logw = -DECAY_SCALE * jax.nn.sigmoid(w0_ref[...] + lora[:, :A_WIDTH])
    lr = jax.nn.sigmoid(a0_ref[...] + lora[:, A_WIDTH:])
    gate = _dot(jax.nn.sigmoid(gate_in).astype(bf16), wgate_ref[...])

    ones_bd = ones_ref[...]
    kk = k * kk_ref[...]
    kk = kk * lax.rsqrt(_group_sum(kk * kk, ones_bd) + L2_EPS)
    kh = k * (1.0 + (lr - 1.0) * ka_ref[...])
    a_s = -kk
    b_s = kk * lr
    bonus = _group_sum(r * kh * rk_ref[...], ones_bd) * v

    ri = lax.broadcasted_iota(jnp.int32, (tc, tc), 0)
    ci = lax.broadcasted_iota(jnp.int32, (tc, tc), 1)
    in_chunk = (ri & (CHUNK - 1)).astype(jnp.uint32)
    dist = (ri - ci).astype(jnp.uint32)
    incl = dist <= in_chunk
    strict = (dist - 1) < in_chunk
    tri = jnp.where(incl, 1.0, 0.0).astype(bf16)
    w_hi, w_mid, w_lo = _split3(logw)
    cs = _dot(tri, w_hi) + (_dot(tri, w_mid) + _dot(tri, w_lo))
    cs3 = cs.reshape(n_chunks, CHUNK, A_WIDTH)
    cs_end = jnp.broadcast_to(cs3[:, CHUNK - 1:CHUNK, :], cs3.shape).reshape(tc, A_WIDTH)

    e_neg = jnp.exp(-cs)
    e_end = jnp.exp(cs_end - cs)
    a_t = a_s * jnp.exp(cs - logw)
    r_t = r * jnp.exp(cs)
    b_t = b_s * e_neg
    k_t = kh * e_neg
    b_h = b_s * e_end
    k_h = kh * e_end
    g_end = jnp.exp(cs_end)

    heads = []
    for hd in range(A_HEADS):
        sl = slice((hd // 2) * LANES, (hd // 2 + 1) * LANES)
        lo = (hd % 2) * A_HEAD_DIM
        hm = (lane >= lo) & (lane < lo + A_HEAD_DIM)
        a_m = jnp.where(hm, a_t[:, sl], 0.0)
        r_m = jnp.where(hm, r_t[:, sl], 0.0)
        v_mb = jnp.where(hm, v[:, sl], 0.0).astype(bf16)
        heads.append(dict(sl=sl, a_m=a_m, r_m=r_m, v_mb=v_mb))
    for hd, st in enumerate(heads):
        sl = st["sl"]
        bk = jnp.concatenate([b_t[:, sl], k_t[:, sl]], axis=0).astype(bf16)
        ar = jnp.concatenate([st["a_m"], st["r_m"]], axis=0).astype(bf16)
        sc = _dot_nt(ar, bk)
        st["l_ab"] = jnp.where(strict, sc[:tc, :tc], 0.0).astype(bf16)
        st["l_ak"] = jnp.where(strict, sc[:tc, tc:], 0.0).astype(bf16)
        st["m_rb"] = jnp.where(incl, sc[tc:, :tc], 0.0).astype(bf16)
        st["m_rk"] = jnp.where(incl, sc[tc:, tc:], 0.0).astype(bf16)
    for st in heads:
        st["x"] = jnp.concatenate([st["a_m"], _dot(st["l_ak"], st["v_mb"])], axis=1)
    for st in heads:
        st["p"] = st["l_ab"]
        st["x"] = st["x"] + _dot(st["p"], st["x"].astype(bf16))
    for _ in range(5):
        for st in heads:
            st["p"] = _dot(st["p"], st["p"]).astype(bf16)
        for st in heads:
            st["x"] = st["x"] + _dot(st["p"], st["x"].astype(bf16))
    for st in heads:
        st["w"] = st["x"][:, :LANES]
        st["u"] = st["x"][:, LANES:]
        rb_wu = _dot(st["m_rb"], st["x"].astype(bf16))
        st["rp"] = st["r_m"] + rb_wu[:, :LANES]
        st["yi"] = rb_wu[:, LANES:] + _dot(st["m_rk"], st["v_mb"])

    qi = lax.broadcasted_iota(jnp.int32, (MXU_DIM, MXU_DIM), 0)
    qj = lax.broadcasted_iota(jnp.int32, (MXU_DIM, MXU_DIM), 1)
    same_head = (qi ^ qj) < A_HEAD_DIM
    n_groups = A_WIDTH // MXU_DIM
    per_group = A_HEADS // n_groups

    def group_cat(key, g):
        hs_g = heads[g * per_group:(g + 1) * per_group]
        pairs = [hs_g[i][key] + hs_g[i + 1][key] for i in range(0, per_group, 2)]
        return jnp.concatenate(pairs, axis=1)

    groups = []
    for g in range(n_groups):
        gs = slice(g * MXU_DIM, (g + 1) * MXU_DIM)
        groups.append(dict(gs=gs, w=group_cat("w", g), u=group_cat("u", g), rp=group_cat("rp", g),
                           yi=group_cat("yi", g), pt=[], qt=[], y=[]))
    for c in range(n_chunks):
        rows = slice(c * CHUNK, (c + 1) * CHUNK)
        for gr in groups:
            gs = gr["gs"]
            bh_c = b_h[rows, gs]
            gr["pt"].append(jnp.where(same_head, _dot(gr["w"][rows].T.astype(bf16), bh_c.astype(bf16)),
                                      0.0).astype(bf16))
            uv = jnp.concatenate([gr["u"][rows], v[rows, gs]], axis=0)
            bkh = jnp.concatenate([bh_c, k_h[rows, gs]], axis=0)
            gr["qt"].append(jnp.where(same_head, _dot(uv.T.astype(bf16), bkh.astype(bf16)), 0.0))
    states = [s_ref[g] for g in range(n_groups)]
    for c in range(n_chunks):
        rows = slice(c * CHUNK, (c + 1) * CHUNK)
        for g, gr in enumerate(groups):
            s_hi, s_lo = _split2(states[g])
            rp_b = gr["rp"][rows].astype(bf16)
            gr["y"].append(_dot_nt(rp_b, s_hi) + _dot_nt(rp_b, s_lo) + gr["yi"][rows])
            states[g] = (states[g] * g_end[c * CHUNK:c * CHUNK + 1, gr["gs"]]
                         + (_dot(s_hi, gr["pt"][c]) + _dot(s_lo, gr["pt"][c])) + gr["qt"][c])
    for g in range(n_groups):
        s_ref[g] = states[g]
    y = jnp.concatenate([jnp.concatenate(gr["y"], axis=0) for gr in groups], axis=1)

    inv_n = 1.0 / A_HEAD_DIM
    mean = _group_sum(y, ones_bd) * inv_n
    d = y - mean
    var = _group_sum(d * d, ones_bd) * inv_n
    yn = d * lax.rsqrt(var + LNX_EPS) * lnw_ref[...] + lnb_ref[...]
    out_ref[0] = ((yn + bonus) * gate).astype(out_ref.dtype)


def _rwkv(za, mu, w0, a0, w_lora, w_gate, k_k, k_a, r_k, ln_w, ln_b, ones_bd):
    bsz, t, _ = za.shape
    tc = min(SCAN_T, t)
    halo_rows = 8
    per = tc // halo_rows
    vec = lambda a: a.reshape(1, -1).astype(f32)
    return pl.pallas_call(
        _rwkv_body,
        grid=(bsz, t // tc),
        in_specs=[
            pl.BlockSpec((1, tc, A_IN), lambda b, i: (b, i, 0)),
            pl.BlockSpec((1, halo_rows, A_IN), lambda b, i: (b, jnp.maximum(i * per - 1, 0), 0)),
            _const_spec((1, A_IN)),
            _const_spec((1, A_WIDTH)),
            _const_spec((1, A_WIDTH)),
            _const_spec((LANES, 2 * A_WIDTH)),
            _const_spec((GATE_LORA, A_WIDTH)),
            _const_spec((1, A_WIDTH)),
            _const_spec((1, A_WIDTH)),
            _const_spec((1, A_WIDTH)),
            _const_spec((1, A_WIDTH)),
            _const_spec((1, A_WIDTH)),
            _const_spec((MXU_DIM, MXU_DIM)),
        ],
        out_specs=pl.BlockSpec((1, tc, A_WIDTH), lambda b, i: (b, i, 0)),
        out_shape=jax.ShapeDtypeStruct((bsz, t, A_WIDTH), bf16),
        scratch_shapes=[pltpu.VMEM((A_WIDTH // MXU_DIM, MXU_DIM, MXU_DIM), f32)],
        compiler_params=pltpu.CompilerParams(
            dimension_semantics=("parallel", "arbitrary"), vmem_limit_bytes=VMEM_LIMIT),
        name="rwkv7",
    )(za, za, vec(mu), vec(w0), vec(a0), w_lora, w_gate, vec(k_k), vec(k_a), vec(r_k),
      vec(ln_w), vec(ln_b), ones_bd)


def _attn_body(bias_ref, q_ref, k_ref, v_ref, diag_ref, lam_ref, sub_ref, o_ref,
               ka_ref, vt_ref, st_ref, m_ref, acc_ref, *, one_minus_lambda_init, lambda_init):
    hd = pl.program_id(1)
    qb = pl.program_id(2)
    bq = q_ref.shape[0]
    t = k_ref.shape[0]
    hw = 2 * B_HEAD_DIM
    n_bias = bias_ref.shape[1]

    @pl.when(qb == 0)
    def _():
        pos = lax.broadcasted_iota(jnp.int32, (t, hw), 0)
        col = lax.broadcasted_iota(jnp.int32, (t, hw), 1)
        pos_cols = jnp.where(col < n_bias, jnp.where((col & 1) == 0, pos >> 8, pos & 255), 0)
        ka_ref[:, :hw] = k_ref[...]
        ka_ref[:, hw:] = pos_cols.astype(f32).astype(bf16)
        vt_ref[:hw, :] = v_ref[...].astype(f32).T.astype(bf16)
        ones_row = lax.broadcasted_iota(jnp.int32, (ATT_ONES_ROWS, t), 0) == 0
        vt_ref[hw:, :] = jnp.where(ones_row, 1.0, 0.0).astype(bf16)

    lane = lax.broadcasted_iota(jnp.int32, (1, hw), 1)
    q = q_ref[...]
    zero = jnp.zeros_like(q)
    q_pos = jnp.zeros((1, hw), f32)
    for i in range(n_bias):
        q_pos = jnp.where(lane == i, bias_ref[hd, i], q_pos)
    q_pos = jnp.broadcast_to(q_pos, (bq, hw)).astype(bf16)
    qs = jnp.concatenate(
        [jnp.concatenate([jnp.where(lane < B_HEAD_DIM, q, zero), q_pos], axis=1),
         jnp.concatenate([jnp.where(lane >= B_HEAD_DIM, q, zero), q_pos], axis=1)], axis=0)

    m_ref[...] = jnp.full_like(m_ref, NEG_BIG)
    acc_ref[...] = jnp.zeros_like(acc_ref)

    def scores(j, slot):
        st_ref[slot] = _dot_nt(ka_ref[pl.ds(pl.multiple_of(j * bq, bq), bq), :], qs)

    def update(start, nkeys, st):
        m_old = m_ref[...]
        m_new = jnp.maximum(m_old, jnp.max(st, axis=0, keepdims=True))
        alpha = jnp.exp2(m_old - m_new)
        p = jnp.exp2(st - m_new).astype(bf16)
        acc_ref[...] = alpha * acc_ref[...] + _dot(vt_ref[:, pl.ds(start, nkeys)], p)
        m_ref[...] = m_new

    def off_diag(j, slot):
        scores(j + 1, 1 - slot)
        update(pl.multiple_of(j * bq, bq), bq, st_ref[slot])

    def off_diag_pair(jj, carry):
        off_diag(2 * jj, 0)
        off_diag(2 * jj + 1, 1)
        return carry

    scores(0, 0)
    lax.fori_loop(0, qb // 2, off_diag_pair, 0)

    @pl.when(qb % 2 == 1)
    def _():
        off_diag(qb - 1, 0)

    sub = diag_ref.shape[2]
    for d in range(bq // sub):
        st = st_ref[qb & 1, d * sub:(d + 1) * sub, :] + diag_ref[0, d]
        update(pl.multiple_of(qb * bq + d * sub, sub), sub, st)

    lq1, lk1, lq2, lk2 = lam_ref[0:1, :], lam_ref[1:2, :], lam_ref[2:3, :], lam_ref[3:4, :]
    lam = (jnp.exp(jnp.sum(lq1 * lk1, axis=-1, keepdims=True))
           - jnp.exp(jnp.sum(lq2 * lk2, axis=-1, keepdims=True)) + lambda_init)
    ot = acc_ref[:hw, :] * (1.0 / acc_ref[hw:hw + 1, :])
    o = (ot[:, :bq] - lam * ot[:, bq:]).T
    ms = jnp.mean(o * o, axis=-1, keepdims=True)
    o = o * lax.rsqrt(ms + NORM_EPS) * sub_ref[...] * one_minus_lambda_init
    o_ref[...] = o.astype(o_ref.dtype)


def _attention(q, k, v, bsz, t, lam_vecs, subln_w, lambda_init):
    bq = min(ATT_BQ, t)
    nq = t // bq
    assert t <= 256 * 256, "key positions are split into two bf16-exact columns"
    slopes = jnp.exp2(-8.0 * jnp.arange(1, B_HEADS + 1, dtype=f32) / B_HEADS) * LOG2_E
    c_hi, c_mid, c_lo = (piece.astype(f32) for piece in _split3(slopes))
    bias_cols = jnp.stack([256.0 * c_hi, c_hi, 256.0 * c_mid, c_mid, 256.0 * c_lo, c_lo], axis=1)
    sub = min(ATT_SUB, bq)
    kk = jnp.arange(bq)[:, None]
    qq = jnp.arange(bq)[None, :]
    later = jnp.where(kk > qq, 2.0 * (qq - kk), 0.0).astype(f32)
    allowed = (kk // CHUNK) <= (qq // CHUNK)
    diag = jnp.where(allowed[None], slopes[:, None, None] * later[None], NEG_BIG)
    diag = jnp.concatenate([diag, diag], axis=2)
    diag = diag.reshape(B_HEADS, bq // sub, sub, 2 * bq)
    hw = 2 * B_HEAD_DIM
    body = functools.partial(_attn_body, one_minus_lambda_init=1.0 - lambda_init, lambda_init=lambda_init)
    return pl.pallas_call(
        body,
        grid=(bsz, B_HEADS, nq),
        in_specs=[
            pl.BlockSpec(memory_space=pltpu.SMEM),
            pl.BlockSpec((bq, hw), lambda b, h, i: (b * nq + i, h)),
            pl.BlockSpec((t, hw), lambda b, h, i: (b, h)),
            pl.BlockSpec((t, hw), lambda b, h, i: (b, h)),
            pl.BlockSpec((1, bq // sub, sub, 2 * bq), lambda b, h, i: (h, 0, 0, 0)),
            pl.BlockSpec((4, B_HEAD_DIM), lambda b, h, i: (0, 0)),
            pl.BlockSpec((1, hw), lambda b, h, i: (0, 0)),
        ],
        out_specs=pl.BlockSpec((bq, hw), lambda b, h, i: (b * nq + i, h)),
        out_shape=jax.ShapeDtypeStruct((bsz * t, B_WIDTH), bf16),
        scratch_shapes=[
            pltpu.VMEM((t, 2 * hw), bf16),
            pltpu.VMEM((hw + ATT_ONES_ROWS, t), bf16),
            pltpu.VMEM((2, bq, 2 * bq), f32),
            pltpu.VMEM((1, 2 * bq), f32),
            pltpu.VMEM((hw + ATT_ONES_ROWS, 2 * bq), f32),
        ],
        compiler_params=pltpu.CompilerParams(
            dimension_semantics=("parallel", "parallel", "arbitrary"), vmem_limit_bytes=VMEM_LIMIT),
        name="diff_attn",
    )(bias_cols, q, k, v, diag, lam_vecs, subln_w.reshape(1, hw).astype(f32))


def _ffn_body(x_ref, ya_ref, yb_ref, wo_ref, nw_ref, wup_ref, cw_ref, cb_ref, wdn_ref, o_ref,
              carry_ref, act_ref, h_ref):
    tb = pl.program_id(1)
    tm = x_ref.shape[1]
    x2 = x_ref[0] + _dot(ya_ref[0], wo_ref[:A_WIDTH, :]) + _dot(yb_ref[0], wo_ref[A_WIDTH:, :])
    ms = jnp.mean(x2 * x2, axis=-1, keepdims=True)
    h_ref[...] = (x2 * lax.rsqrt(ms + NORM_EPS) * nw_ref[...]).astype(bf16)

    @pl.when(tb == 0)
    def _():
        carry_ref[...] = jnp.zeros_like(carry_ref)

    def conv(u, col):
        cols = slice(col, col + FFN_TF)
        old = carry_ref[:, cols]
        carry_ref[:, cols] = u[tm - 8:, :]
        ext = jnp.concatenate([old, u], axis=0)
        return (cw_ref[0:1, cols] * ext[6:6 + tm] + cw_ref[1:2, cols] * ext[7:7 + tm]
                + cw_ref[2:3, cols] * u + cb_ref[:, cols])

    for j in range(D_FF // FFN_TF):
        gcol = j * FFN_TF
        ucol = D_FF + j * FFN_TF
        gate = conv(_dot(h_ref[...], wup_ref[:, gcol:gcol + FFN_TF]), gcol)
        up = conv(_dot(h_ref[...], wup_ref[:, ucol:ucol + FFN_TF]), ucol)
        act_ref[:, gcol:gcol + FFN_TF] = (gate * jax.nn.sigmoid(gate) * up).astype(bf16)
    o_ref[0] = x2 + _dot(act_ref[...], wdn_ref[...])


def _out_ffn(x, ya, yb, w_out_bf, norm_w, w_up_bf, conv_w, conv_b, w_down_bf):
    bsz, t, _ = x.shape
    tm = min(FFN_TM, t)
    tok = lambda b, i: (b, i, 0)
    return pl.pallas_call(
        _ffn_body,
        grid=(bsz, t // tm),
        in_specs=[
            pl.BlockSpec((1, tm, D_MODEL), tok),
            pl.BlockSpec((1, tm, A_WIDTH), tok),
            pl.BlockSpec((1, tm, B_WIDTH), tok),
            _const_spec((A_WIDTH + B_WIDTH, D_MODEL)),
            _const_spec((1, D_MODEL)),
            _const_spec((D_MODEL, 2 * D_FF)),
            _const_spec((3, 2 * D_FF)),
            _const_spec((1, 2 * D_FF)),
            _const_spec((D_FF, D_MODEL)),
        ],
        out_specs=pl.BlockSpec((1, tm, D_MODEL), tok),
        out_shape=jax.ShapeDtypeStruct((bsz, t, D_MODEL), f32),
        scratch_shapes=[
            pltpu.VMEM((8, 2 * D_FF), f32),
            pltpu.VMEM((tm, D_FF), bf16),
            pltpu.VMEM((tm, D_MODEL), bf16),
        ],
        compiler_params=pltpu.CompilerParams(
            dimension_semantics=("parallel", "arbitrary"), vmem_limit_bytes=VMEM_LIMIT),
        name="out_ffn",
    )(x, ya, yb, w_out_bf, norm_w, w_up_bf, conv_w, conv_b, w_down_bf)


def _layer(x, lambda_init, attn_norm_w, w_in, mu_shift, w0, w_decay_up, a0, w_aaa_up, w_gate_up,
           k_k, k_a, r_k, ln_x_w, ln_x_b, q_norm_w, k_norm_w, lambda_q1, lambda_k1, lambda_q2,
           lambda_k2, subln_w, w_out, ffn_norm_w, w_ffn_up, ffn_conv_w, ffn_conv_b, w_ffn_down):
    bsz, t, _ = x.shape
    head_ids = jnp.arange(MXU_DIM) // A_HEAD_DIM
    ones_bd = (head_ids[:, None] == head_ids[None, :]).astype(bf16)
    scale = LOG2_E / math.sqrt(B_HEAD_DIM)
    qk_w = jnp.stack([jnp.tile(q_norm_w.astype(f32), B_WIDTH // B_HEAD_DIM) * scale,
                      jnp.tile(k_norm_w.astype(f32), B_WIDTH // B_HEAD_DIM)])
    za, q, k, v = _inproj(x.reshape(bsz * t, D_MODEL), attn_norm_w.reshape(1, D_MODEL),
                          w_in.astype(bf16), qk_w, ones_bd)

    zeros = jnp.zeros((DECAY_LORA, A_WIDTH), f32)
    w_lora = jnp.concatenate([jnp.concatenate([w_decay_up, zeros], axis=1),
                              jnp.concatenate([zeros, w_aaa_up], axis=1)], axis=0).astype(bf16)
    ya = _rwkv(za.reshape(bsz, t, A_IN), mu_shift, w0, a0, w_lora, w_gate_up.astype(bf16),
               k_k, k_a, r_k, ln_x_w, ln_x_b, ones_bd)

    lam_vecs = jnp.stack([lambda_q1, lambda_k1, lambda_q2, lambda_k2]).astype(f32)
    yb = _attention(q, k, v, bsz, t, lam_vecs, subln_w, lambda_init)

    return _out_ffn(x, ya, yb.reshape(bsz, t, B_WIDTH), w_out.astype(bf16),
                    ffn_norm_w.reshape(1, D_MODEL), w_ffn_up.astype(bf16), ffn_conv_w,
                    ffn_conv_b.reshape(1, 2 * D_FF), w_ffn_down.astype(bf16))


def kernel(x, attn_norm_w, w_in, mu_shift, w0, w_decay_up, a0, w_aaa_up, w_gate_up, k_k, k_a, r_k,
           ln_x_w, ln_x_b, q_norm_w, k_norm_w, lambda_q1, lambda_k1, lambda_q2, lambda_k2, subln_w,
           w_out, ffn_norm_w, w_ffn_up, ffn_conv_w, ffn_conv_b, w_ffn_down):
    params = (attn_norm_w, w_in, mu_shift, w0, w_decay_up, a0, w_aaa_up, w_gate_up, k_k, k_a, r_k,
              ln_x_w, ln_x_b, q_norm_w, k_norm_w, lambda_q1, lambda_k1, lambda_q2, lambda_k2,
              subln_w, w_out, ffn_norm_w, w_ffn_up, ffn_conv_w, ffn_conv_b, w_ffn_down)
    for l in range(attn_norm_w.shape[0]):
        lambda_init = 0.8 - 0.6 * math.exp(-0.3 * l)
        x = _layer(x, lambda_init, *(p[l] for p in params))
    return x
```

```python
import functools
import math

import jax
import jax.numpy as jnp
from jax import lax
from jax.experimental import pallas as pl
from jax.experimental.pallas import tpu as pltpu

f32 = jnp.float32
bf16 = jnp.bfloat16

D_MODEL = 1024
CHUNK = 64
A_HEADS = 8
A_HEAD_DIM = 64
A_WIDTH = A_HEADS * A_HEAD_DIM
DECAY_LORA = 64
AAA_LORA = 64
GATE_LORA = 128
A_IN = 3 * A_WIDTH + DECAY_LORA + AAA_LORA + GATE_LORA
B_HEADS = 4
B_HEAD_DIM = 64
B_WIDTH = B_HEADS * 2 * B_HEAD_DIM
B_IN = 3 * B_WIDTH
IN_WIDTH = A_IN + B_IN
D_FF = 2816
NORM_EPS = 1e-6
LNX_EPS = 64e-5
DECAY_SCALE = math.exp(-0.5)
L2_EPS = 1e-12

LANES = 128
MXU_DIM = 256
VMEM_LIMIT = 56 * 1024 * 1024

IN_TM = 512
SCAN_T = 256
ATT_BQ = 512
ATT_ONES_ROWS = 16
LOG2_E = math.log2(math.e)
FFN_TM = 512
FFN_TF = 256
NEG_BIG = -1e30


def _dot(a, b, dims=((1,), (0,))):
    return lax.dot_general(a, b, (dims, ((), ())), preferred_element_type=f32)


def _dot_nt(a, b):
    return _dot(a, b, ((1,), (1,)))


def _split2(x):
    hi = x.astype(bf16)
    lo = (x - hi.astype(f32)).astype(bf16)
    return hi, lo


def _split3(x):
    hi = x.astype(bf16)
    r1 = x - hi.astype(f32)
    mid = r1.astype(bf16)
    lo = (r1 - mid.astype(f32)).astype(bf16)
    return hi, mid, lo


def _group_sum(x, ones_bd):
    xb = x.astype(bf16)
    parts = []
    for g in range(x.shape[1] // MXU_DIM):
        cols = slice(g * MXU_DIM, (g + 1) * MXU_DIM)
        parts.append(_dot(xb[:, cols], ones_bd))
    return jnp.concatenate(parts, axis=1)


def _const_spec(shape):
    nd = len(shape)
    return pl.BlockSpec(shape, lambda *_: (0,) * nd, pipeline_mode=pl.Buffered(1))


def _inproj_body(x_ref, nw_ref, w_ref, qkw_ref, ones_ref, za_ref, q_ref, k_ref, v_ref):
    x = x_ref[...]
    ms = jnp.mean(x * x, axis=-1, keepdims=True)
    h = (x * lax.rsqrt(ms + NORM_EPS) * nw_ref[...]).astype(bf16)
    za_ref[...] = _dot(h, w_ref[:, :A_IN])
    ones_bd = ones_ref[...]

    def qk_norm(col0, row):
        z = _dot(h, w_ref[:, col0:col0 + B_WIDTH])
        ss = _group_sum(z * z, ones_bd)
        return (z * lax.rsqrt(ss * (1.0 / B_HEAD_DIM) + NORM_EPS) * qkw_ref[row:row + 1, :]).astype(bf16)

    q_ref[...] = qk_norm(A_IN, 0)
    k_ref[...] = qk_norm(A_IN + B_WIDTH, 1)
    v_ref[...] = _dot(h, w_ref[:, A_IN + 2 * B_WIDTH:]).astype(bf16)


def _inproj(x2d, norm_w, w_in_bf, qk_w, ones_bd):
    n = x2d.shape[0]
    tm = min(IN_TM, n)
    row = lambda i: (i, 0)
    return pl.pallas_call(
        _inproj_body,
        grid=(n // tm,),
        in_specs=[
            pl.BlockSpec((tm, D_MODEL), row),
            _const_spec((1, D_MODEL)),
            _const_spec((D_MODEL, IN_WIDTH)),
            _const_spec((2, B_WIDTH)),
            _const_spec((MXU_DIM, MXU_DIM)),
        ],
        out_specs=[
            pl.BlockSpec((tm, A_IN), row),
            pl.BlockSpec((tm, B_WIDTH), row),
            pl.BlockSpec((tm, B_WIDTH), row),
            pl.BlockSpec((tm, B_WIDTH), row),
        ],
        out_shape=[
            jax.ShapeDtypeStruct((n, A_IN), f32),
            jax.ShapeDtypeStruct((n, B_WIDTH), bf16),
            jax.ShapeDtypeStruct((n, B_WIDTH), bf16),
            jax.ShapeDtypeStruct((n, B_WIDTH), bf16),
        ],
        compiler_params=pltpu.CompilerParams(
            dimension_semantics=("parallel",), vmem_limit_bytes=VMEM_LIMIT),
        name="inproj",
    )(x2d, norm_w, w_in_bf, qk_w, ones_bd)


def _rwkv_body(za_ref, halo_ref, mu_ref, w0_ref, a0_ref, wlora_ref, wgate_ref, kk_ref, ka_ref,
               rk_ref, lnw_ref, lnb_ref, ones_ref, out_ref, s_ref):
    tb = pl.program_id(1)
    tc = za_ref.shape[1]
    n_chunks = tc // CHUNK

    @pl.when(tb == 0)
    def _():
        s_ref[...] = jnp.zeros_like(s_ref)

    h = za_ref[0]
    last = jnp.where(tb == 0, 0.0, halo_ref[0][7:8, :])
    row0 = lax.broadcasted_iota(jnp.int32, (tc, 1), 0) == 0
    prev = jnp.where(row0, last, pltpu.roll(h, 1, axis=0))
    hs = h + (prev - h) * mu_ref[...]

    r = hs[:, 0:A_WIDTH]
    k = hs[:, A_WIDTH:2 * A_WIDTH]
    v = hs[:, 2 * A_WIDTH:3 * A_WIDTH]
    lora_in = hs[:, 3 * A_WIDTH:3 * A_WIDTH + LANES]
    gate_in = hs[:, 3 * A_WIDTH + LANES:]
    lane = lax.broadcasted_iota(jnp.int32, (1, LANES), 1)
    lora_act = jnp.where(lane < DECAY_LORA, jnp.tanh(lora_in), lora_in).astype(bf16)
    lora = _dot(lora_act, wlora_ref[...])
    logw = -DECAY_SCALE * jax.nn.sigmoid(w0_ref[...] + lora[:, :A_WIDTH])
    lr = jax.nn.sigmoid(a0_ref[...] + lora[:, A_WIDTH:])
    gate = _dot(jax.nn.sigmoid(gate_in).astype(bf16), wgate_ref[...])

    ones_bd = ones_ref[...]
    kk = k * kk_ref[...]
    kk = kk * lax.rsqrt(_group_sum(kk * kk, ones_bd) + L2_EPS)
    kh = k * (1.0 + (lr - 1.0) * ka_ref[...])
    a_s = -kk
    b_s = kk * lr
    bonus = _group_sum(r * kh * rk_ref[...], ones_bd) * v

    ri = lax.broadcasted_iota(jnp.int32, (tc, tc), 0)
    ci = lax.broadcasted_iota(jnp.int32, (tc, tc), 1)
    in_chunk = (ri & (CHUNK - 1)).astype(jnp.uint32)
    dist = (ri - ci).astype(jnp.uint32)
    incl = dist <= in_chunk
    strict = (dist - 1) < in_chunk
    tri = jnp.where(incl, 1.0, 0.0).astype(bf16)
    w_hi, w_lo = _split2(logw)
    cs = _dot(tri, w_hi) + _dot(tri, w_lo)
    cs3 = cs.reshape(n_chunks, CHUNK, A_WIDTH)
    cs_end = jnp.broadcast_to(cs3[:, CHUNK - 1:CHUNK, :], cs3.shape).reshape(tc, A_WIDTH)

    e_neg = jnp.exp(-cs)
    e_end = jnp.exp(cs_end - cs)
    a_t = a_s * jnp.exp(cs - logw)
    r_t = r * jnp.exp(cs)
    b_t = b_s * e_neg
    k_t = kh * e_neg
    b_h = b_s * e_end
    k_h = kh * e_end
    g_end = jnp.exp(cs_end)

    heads = []
    for hd in range(A_HEADS):
        sl = slice((hd // 2) * LANES, (hd // 2 + 1) * LANES)
        lo = (hd % 2) * A_HEAD_DIM
        hm = (lane >= lo) & (lane < lo + A_HEAD_DIM)
        a_m = jnp.where(hm, a_t[:, sl], 0.0)
        r_m = jnp.where(hm, r_t[:, sl], 0.0)
        v_mb = jnp.where(hm, v[:, sl], 0.0).astype(bf16)
        heads.append(dict(sl=sl, a_m=a_m, r_m=r_m, v_mb=v_mb))
    for hd, st in enumerate(heads):
        sl = st["sl"]
        bk = jnp.concatenate([b_t[:, sl], k_t[:, sl]], axis=0).astype(bf16)
        ar = jnp.concatenate([st["a_m"], st["r_m"]], axis=0).astype(bf16)
        sc = _dot_nt(ar, bk)
        st["l_ab"] = jnp.where(strict, sc[:tc, :tc], 0.0).astype(bf16)
        st["l_ak"] = jnp.where(strict, sc[:tc, tc:], 0.0).astype(bf16)
        st["m_rb"] = jnp.where(incl, sc[tc:, :tc], 0.0).astype(bf16)
        st["m_rk"] = jnp.where(incl, sc[tc:, tc:], 0.0).astype(bf16)
    for st in heads:
        st["x"] = jnp.concatenate([st["a_m"], _dot(st["l_ak"], st["v_mb"])], axis=1)
    for st in heads:
        st["p"] = st["l_ab"]
        st["x"] = st["x"] + _dot(st["p"], st["x"].astype(bf16))
    for _ in range(5):
        for st in heads:
            st["p"] = _dot(st["p"], st["p"]).astype(bf16)
        for st in heads:
            st["x"] = st["x"] + _dot(st["p"], st["x"].astype(bf16))
    for st in heads:
        st["w"] = st["x"][:, :LANES]
        st["u"] = st["x"][:, LANES:]
        rb_wu = _dot(st["m_rb"], st["x"].astype(bf16))
        st["rp"] = st["r_m"] + rb_wu[:, :LANES]
        st["yi"] = rb_wu[:, LANES:] + _dot(st["m_rk"], st["v_mb"])

    qi = lax.broadcasted_iota(jnp.int32, (MXU_DIM, MXU_DIM), 0)
    qj = lax.broadcasted_iota(jnp.int32, (MXU_DIM, MXU_DIM), 1)
    same_head = (qi ^ qj) < A_HEAD_DIM
    n_groups = A_WIDTH // MXU_DIM
    per_group = A_HEADS // n_groups

    def group_cat(key, g):
        hs_g = heads[g * per_group:(g + 1) * per_group]
        pairs = [hs_g[i][key] + hs_g[i + 1][key] for i in range(0, per_group, 2)]
        return jnp.concatenate(pairs, axis=1)

    groups = []
    for g in range(n_groups):
        gs = slice(g * MXU_DIM, (g + 1) * MXU_DIM)
        groups.append(dict(gs=gs, w=group_cat("w", g), u=group_cat("u", g), rp=group_cat("rp", g),
                           yi=group_cat("yi", g), pt=[], qt=[], y=[]))
    for c in range(n_chunks):
        rows = slice(c * CHUNK, (c + 1) * CHUNK)
        for gr in groups:
            gs = gr["gs"]
            bh_c = b_h[rows, gs]
            gr["pt"].append(jnp.where(same_head, _dot(gr["w"][rows].T.astype(bf16), bh_c.astype(bf16)),
                                      0.0).astype(bf16))
            uv = jnp.concatenate([gr["u"][rows], v[rows, gs]], axis=0)
            bkh = jnp.concatenate([bh_c, k_h[rows, gs]], axis=0)
            gr["qt"].append(jnp.where(same_head, _dot(uv.T.astype(bf16), bkh.astype(bf16)), 0.0))
    states = [s_ref[g] for g in range(n_groups)]
    for c in range(n_chunks):
        rows = slice(c * CHUNK, (c + 1) * CHUNK)
        for g, gr in enumerate(groups):
            s_b = states[g].astype(bf16)
            gr["y"].append(_dot_nt(gr["rp"][rows].astype(bf16), s_b) + gr["yi"][rows])
            states[g] = (states[g] * g_end[c * CHUNK:c * CHUNK + 1, gr["gs"]]
                         + _dot(s_b, gr["pt"][c]) + gr["qt"][c])
    for g in range(n_groups):
        s_ref[g] = states[g]
    y = jnp.concatenate([jnp.concatenate(gr["y"], axis=0) for gr in groups], axis=1)

    inv_n = 1.0 / A_HEAD_DIM
    mean = _group_sum(y, ones_bd) * inv_n
    d = y - mean
    var = _group_sum(d * d, ones_bd) * inv_n
    yn = d * lax.rsqrt(var + LNX_EPS) * lnw_ref[...] + lnb_ref[...]
    out_ref[0] = ((yn + bonus) * gate).astype(out_ref.dtype)


def _rwkv(za, mu, w0, a0, w_lora, w_gate, k_k, k_a, r_k, ln_w, ln_b, ones_bd):
    bsz, t, _ = za.shape
    tc = min(SCAN_T, t)
    halo_rows = 8
    per = tc // halo_rows
    vec = lambda a: a.reshape(1, -1).astype(f32)
    return pl.pallas_call(
        _rwkv_body,
        grid=(bsz, t // tc),
        in_specs=[
            pl.BlockSpec((1, tc, A_IN), lambda b, i: (b, i, 0)),
            pl.BlockSpec((1, halo_rows, A_IN), lambda b, i: (b, jnp.maximum(i * per - 1, 0), 0)),
            _const_spec((1, A_IN)),
            _const_spec((1, A_WIDTH)),
            _const_spec((1, A_WIDTH)),
            _const_spec((LANES, 2 * A_WIDTH)),
            _const_spec((GATE_LORA, A_WIDTH)),
            _const_spec((1, A_WIDTH)),
            _const_spec((1, A_WIDTH)),
            _const_spec((1, A_WIDTH)),
            _const_spec((1, A_WIDTH)),
            _const_spec((1, A_WIDTH)),
            _const_spec((MXU_DIM, MXU_DIM)),
        ],
        out_specs=pl.BlockSpec((1, tc, A_WIDTH), lambda b, i: (b, i, 0)),
        out_shape=jax.ShapeDtypeStruct((bsz, t, A_WIDTH), bf16),
        scratch_shapes=[pltpu.VMEM((A_WIDTH // MXU_DIM, MXU_DIM, MXU_DIM), f32)],
        compiler_params=pltpu.CompilerParams(
            dimension_semantics=("parallel", "arbitrary"), vmem_limit_bytes=VMEM_LIMIT),
        name="rwkv7",
    )(za, za, vec(mu), vec(w0), vec(a0), w_lora, w_gate, vec(k_k), vec(k_a), vec(r_k),
      vec(ln_w), vec(ln_b), ones_bd)


def _attn_body(bias_ref, q_ref, k_ref, v_ref, diag_ref, lam_ref, sub_ref, o_ref,
               ka_ref, vt_ref, qs_ref, st_ref, mx_ref, m_ref, acc_ref, *, one_minus_lambda_init, lambda_init):
    hd = pl.program_id(1)
    phase = pl.program_id(2)
    t = k_ref.shape[0]
    bq = diag_ref.shape[1]
    hw = 2 * B_HEAD_DIM
    n_bias = bias_ref.shape[1]
    lane = lax.broadcasted_iota(jnp.int32, (1, hw), 1)

    @pl.when(phase == 0)
    def _():
        pos = lax.broadcasted_iota(jnp.int32, (t, hw), 0)
        col = lax.broadcasted_iota(jnp.int32, (t, hw), 1)
        pos_cols = jnp.where(col < n_bias, jnp.where((col & 1) == 0, pos >> 8, pos & 255), 0)
        ka_ref[:, :hw] = k_ref[...]
        ka_ref[:, hw:] = pos_cols.astype(f32).astype(bf16)
        vt_ref[:hw, :] = v_ref[...].astype(f32).T.astype(bf16)
        ones_row = lax.broadcasted_iota(jnp.int32, (ATT_ONES_ROWS, t), 0) == 0
        vt_ref[hw:, :] = jnp.where(ones_row, 1.0, 0.0).astype(bf16)
        q_pos = jnp.zeros((1, hw), f32)
        for i in range(n_bias):
            q_pos = jnp.where(lane == i, bias_ref[hd, i], q_pos)
        q_pos = jnp.broadcast_to(q_pos, (bq, hw)).astype(bf16)
        for i in range(t // bq):
            q = q_ref[i * bq:(i + 1) * bq, :]
            zero = jnp.zeros_like(q)
            qs_ref[i, :bq, :hw] = jnp.where(lane < B_HEAD_DIM, q, zero)
            qs_ref[i, bq:, :hw] = jnp.where(lane >= B_HEAD_DIM, q, zero)
            qs_ref[i, :bq, hw:] = q_pos
            qs_ref[i, bq:, hw:] = q_pos

    @pl.when(phase == 1)
    def _():
        _attn_tiles(q_ref, diag_ref, lam_ref, sub_ref, o_ref, ka_ref, vt_ref, qs_ref, st_ref, mx_ref,
                    m_ref, acc_ref, one_minus_lambda_init, lambda_init)


def _attn_tiles(q_ref, diag_ref, lam_ref, sub_ref, o_ref, ka_ref, vt_ref, qs_ref, st_ref, mx_ref,
                m_ref, acc_ref, one_minus_lambda_init, lambda_init):
    t = ka_ref.shape[0]
    bq = diag_ref.shape[1]
    hw = 2 * B_HEAD_DIM

    lq1, lk1, lq2, lk2 = lam_ref[0:1, :], lam_ref[1:2, :], lam_ref[2:3, :], lam_ref[3:4, :]
    lam = (jnp.exp(jnp.sum(lq1 * lk1, axis=-1, keepdims=True))
           - jnp.exp(jnp.sum(lq2 * lk2, axis=-1, keepdims=True)) + lambda_init)

    def scores(i, j, slot):
        st = _dot_nt(ka_ref[j * bq:(j + 1) * bq, :], qs_ref[i])
        if j == i:
            st = st + diag_ref[0]
        st_ref[slot] = st
        mx_ref[slot] = jnp.max(st, axis=0, keepdims=True)

    def update(j, slot):
        m_old = m_ref[...]
        m_new = jnp.maximum(m_old, mx_ref[slot])
        alpha = jnp.exp2(m_old - m_new)
        p = jnp.exp2(st_ref[slot] - m_new).astype(bf16)
        acc_ref[...] = alpha * acc_ref[...] + _dot(vt_ref[:, j * bq:(j + 1) * bq], p)
        m_ref[...] = m_new

    def finalize(i):
        ot = acc_ref[:hw, :] * (1.0 / acc_ref[hw:hw + 1, :])
        o = (ot[:, :bq] - lam * ot[:, bq:]).T
        ms = jnp.mean(o * o, axis=-1, keepdims=True)
        o = o * lax.rsqrt(ms + NORM_EPS) * sub_ref[...] * one_minus_lambda_init
        o_ref[i * bq:(i + 1) * bq, :] = o.astype(o_ref.dtype)

    tiles = [(i, j) for i in range(t // bq) for j in range(i + 1)]
    scores(*tiles[0], 0)
    for n, (i, j) in enumerate(tiles):
        slot = n % 2
        if n + 1 < len(tiles):
            scores(*tiles[n + 1], 1 - slot)
        if j == 0:
            m_ref[...] = jnp.full_like(m_ref, NEG_BIG)
            acc_ref[...] = jnp.zeros_like(acc_ref)
        update(j, slot)
        if j == i:
            finalize(i)


def _attention(q, k, v, bsz, t, lam_vecs, subln_w, lambda_init):
    bq = min(ATT_BQ, t)
    nq = t // bq
    assert t <= 256 * 256, "key positions are split into two bf16-exact columns"
    slopes = jnp.exp2(-8.0 * jnp.arange(1, B_HEADS + 1, dtype=f32) / B_HEADS) * LOG2_E
    c_hi, c_mid, c_lo = (piece.astype(f32) for piece in _split3(slopes))
    bias_cols = jnp.stack([256.0 * c_hi, c_hi, 256.0 * c_mid, c_mid, 256.0 * c_lo, c_lo], axis=1)
    kk = jnp.arange(bq)[:, None]
    qq = jnp.arange(bq)[None, :]
    later = jnp.where(kk > qq, 2.0 * (qq - kk), 0.0).astype(f32)
    allowed = (kk // CHUNK) <= (qq // CHUNK)
    diag = jnp.where(allowed[None], slopes[:, None, None] * later[None], NEG_BIG)
    diag = jnp.concatenate([diag, diag], axis=2)
    hw = 2 * B_HEAD_DIM
    body = functools.partial(_attn_body, one_minus_lambda_init=1.0 - lambda_init, lambda_init=lambda_init)
    head_block = lambda: pl.BlockSpec((t, hw), lambda b, h, p: (b, h))
    return pl.pallas_call(
        body,
        grid=(bsz, B_HEADS, 2),
        in_specs=[
            pl.BlockSpec(memory_space=pltpu.SMEM),
            head_block(),
            head_block(),
            head_block(),
            pl.BlockSpec((1, bq, 2 * bq), lambda b, h, p: (h, 0, 0)),
            pl.BlockSpec((4, B_HEAD_DIM), lambda b, h, p: (0, 0)),
            pl.BlockSpec((1, hw), lambda b, h, p: (0, 0)),
        ],
        out_specs=head_block(),
        out_shape=jax.ShapeDtypeStruct((bsz * t, B_WIDTH), bf16),
        scratch_shapes=[
            pltpu.VMEM((t, 2 * hw), bf16),
            pltpu.VMEM((hw + ATT_ONES_ROWS, t), bf16),
            pltpu.VMEM((nq, 2 * bq, 2 * hw), bf16),
            pltpu.VMEM((2, bq, 2 * bq), f32),
            pltpu.VMEM((2, 1, 2 * bq), f32),
            pltpu.VMEM((1, 2 * bq), f32),
            pltpu.VMEM((hw + ATT_ONES_ROWS, 2 * bq), f32),
        ],
        compiler_params=pltpu.CompilerParams(
            dimension_semantics=("parallel", "parallel", "arbitrary"), vmem_limit_bytes=VMEM_LIMIT),
        name="diff_attn",
    )(bias_cols, q, k, v, diag, lam_vecs, subln_w.reshape(1, hw).astype(f32))


def _ffn_body(x_ref, ya_ref, yb_ref, wo_ref, nw_ref, wup_ref, cw_ref, cb_ref, wdn_ref, o_ref,
              carry_ref, act_ref, h_ref):
    tb = pl.program_id(1)
    tm = x_ref.shape[1]
    x2 = x_ref[0] + _dot(ya_ref[0], wo_ref[:A_WIDTH, :]) + _dot(yb_ref[0], wo_ref[A_WIDTH:, :])
    ms = jnp.mean(x2 * x2, axis=-1, keepdims=True)
    h_ref[...] = (x2 * lax.rsqrt(ms + NORM_EPS) * nw_ref[...]).astype(bf16)

    @pl.when(tb == 0)
    def _():
        carry_ref[...] = jnp.zeros_like(carry_ref)

    def conv(u, col):
        cols = slice(col, col + FFN_TF)
        old = carry_ref[:, cols]
        carry_ref[:, cols] = u[tm - 8:, :]
        ext = jnp.concatenate([old, u], axis=0)
        return (cw_ref[0:1, cols] * ext[6:6 + tm] + cw_ref[1:2, cols] * ext[7:7 + tm]
                + cw_ref[2:3, cols] * u + cb_ref[:, cols])

    for j in range(D_FF // FFN_TF):
        gcol = j * FFN_TF
        ucol = D_FF + j * FFN_TF
        gate = conv(_dot(h_ref[...], wup_ref[:, gcol:gcol + FFN_TF]), gcol)
        up = conv(_dot(h_ref[...], wup_ref[:, ucol:ucol + FFN_TF]), ucol)
        act_ref[:, gcol:gcol + FFN_TF] = (gate * jax.nn.sigmoid(gate) * up).astype(bf16)
    o_ref[0] = x2 + _dot(act_ref[...], wdn_ref[...])


def _out_ffn(x, ya, yb, w_out_bf, norm_w, w_up_bf, conv_w, conv_b, w_down_bf):
    bsz, t, _ = x.shape
    tm = min(FFN_TM, t)
    tok = lambda b, i: (b, i, 0)
    return pl.pallas_call(
        _ffn_body,
        grid=(bsz, t // tm),
        in_specs=[
            pl.BlockSpec((1, tm, D_MODEL), tok),
            pl.BlockSpec((1, tm, A_WIDTH), tok),
            pl.BlockSpec((1, tm, B_WIDTH), tok),
            _const_spec((A_WIDTH + B_WIDTH, D_MODEL)),
            _const_spec((1, D_MODEL)),
            _const_spec((D_MODEL, 2 * D_FF)),
            _const_spec((3, 2 * D_FF)),
            _const_spec((1, 2 * D_FF)),
            _const_spec((D_FF, D_MODEL)),
        ],
        out_specs=pl.BlockSpec((1, tm, D_MODEL), tok),
        out_shape=jax.ShapeDtypeStruct((bsz, t, D_MODEL), f32),
        scratch_shapes=[
            pltpu.VMEM((8, 2 * D_FF), f32),
            pltpu.VMEM((tm, D_FF), bf16),
            pltpu.VMEM((tm, D_MODEL), bf16),
        ],
        compiler_params=pltpu.CompilerParams(
            dimension_semantics=("parallel", "arbitrary"), vmem_limit_bytes=VMEM_LIMIT),
        name="out_ffn",
    )(x, ya, yb, w_out_bf, norm_w, w_up_bf, conv_w, conv_b, w_down_bf)


def _layer(x, lambda_init, attn_norm_w, w_in, mu_shift, w0, w_decay_up, a0, w_aaa_up, w_gate_up,
           k_k, k_a, r_k, ln_x_w, ln_x_b, q_norm_w, k_norm_w, lambda_q1, lambda_k1, lambda_q2,
           lambda_k2, subln_w, w_out, ffn_norm_w, w_ffn_up, ffn_conv_w, ffn_conv_b, w_ffn_down):
    bsz, t, _ = x.shape
    head_ids = jnp.arange(MXU_DIM) // A_HEAD_DIM
    ones_bd = (head_ids[:, None] == head_ids[None, :]).astype(bf16)
    scale = LOG2_E / math.sqrt(B_HEAD_DIM)
    qk_w = jnp.stack([jnp.tile(q_norm_w.astype(f32), B_WIDTH // B_HEAD_DIM) * scale,
                      jnp.tile(k_norm_w.astype(f32), B_WIDTH // B_HEAD_DIM)])
    za, q, k, v = _inproj(x.reshape(bsz * t, D_MODEL), attn_norm_w.reshape(1, D_MODEL),
                          w_in.astype(bf16), qk_w, ones_bd)

    zeros = jnp.zeros((DECAY_LORA, A_WIDTH), f32)
    w_lora = jnp.concatenate([jnp.concatenate([w_decay_up, zeros], axis=1),
                              jnp.concatenate([zeros, w_aaa_up], axis=1)], axis=0).astype(bf16)
    ya = _rwkv(za.reshape(bsz, t, A_IN), mu_shift, w0, a0, w_lora, w_gate_up.astype(bf16),
               k_k, k_a, r_k, ln_x_w, ln_x_b, ones_bd)

    lam_vecs = jnp.stack([lambda_q1, lambda_k1, lambda_q2, lambda_k2]).astype(f32)
    yb = _attention(q, k, v, bsz, t, lam_vecs, subln_w, lambda_init)

    return _out_ffn(x, ya, yb.reshape(bsz, t, B_WIDTH), w_out.astype(bf16),
                    ffn_norm_w.reshape(1, D_MODEL), w_ffn_up.astype(bf16), ffn_conv_w,
                    ffn_conv_b.reshape(1, 2 * D_FF), w_ffn_down.astype(bf16))


def kernel(x, attn_norm_w, w_in, mu_shift, w0, w_decay_up, a0, w_aaa_up, w_gate_up, k_k, k_a, r_k,
           ln_x_w, ln_x_b, q_norm_w, k_norm_w, lambda_q1, lambda_k1, lambda_q2, lambda_k2, subln_w,
           w_out, ffn_norm_w, w_ffn_up, ffn_conv_w, ffn_conv_b, w_ffn_down):
    params = (attn_norm_w, w_in, mu_shift, w0, w_decay_up, a0, w_aaa_up, w_gate_up, k_k, k_a, r_k,
              ln_x_w, ln_x_b, q_norm_w, k_norm_w, lambda_q1, lambda_k1, lambda_q2, lambda_k2,
              subln_w, w_out, ffn_norm_w, w_ffn_up, ffn_conv_w, ffn_conv_b, w_ffn_down)
    for l in range(attn_norm_w.shape[0]):
        lambda_init = 0.8 - 0.6 * math.exp(-0.3 * l)
        x = _layer(x, lambda_init, *(p[l] for p in params))
    return x
```

```python
import functools
import math

import jax
import jax.numpy as jnp
from jax import lax
from jax.experimental import pallas as pl
from jax.experimental.pallas import tpu as pltpu

f32 = jnp.float32
bf16 = jnp.bfloat16

D_MODEL = 1024
CHUNK = 64
A_HEADS = 8
A_HEAD_DIM = 64
A_WIDTH = A_HEADS * A_HEAD_DIM
DECAY_LORA = 64
AAA_LORA = 64
GATE_LORA = 128
A_IN = 3 * A_WIDTH + DECAY_LORA + AAA_LORA + GATE_LORA
B_HEADS = 4
B_HEAD_DIM = 64
B_WIDTH = B_HEADS * 2 * B_HEAD_DIM
B_IN = 3 * B_WIDTH
IN_WIDTH = A_IN + B_IN
D_FF = 2816
NORM_EPS = 1e-6
LNX_EPS = 64e-5
DECAY_SCALE = math.exp(-0.5)
L2_EPS = 1e-12

LANES = 128
MXU_DIM = 256
VMEM_LIMIT = 56 * 1024 * 1024

IN_TM = 512
SCAN_T = 256
ATT_BQ = 512
ATT_ONES_ROWS = 16
LOG2_E = math.log2(math.e)
FFN_TM = 512
FFN_TF = 256
NEG_BIG = -1e30


def _dot(a, b, dims=((1,), (0,))):
    return lax.dot_general(a, b, (dims, ((), ())), preferred_element_type=f32)


def _dot_nt(a, b):
    return _dot(a, b, ((1,), (1,)))


def _split2(x):
    hi = x.astype(bf16)
    lo = (x - hi.astype(f32)).astype(bf16)
    return hi, lo


def _split3(x):
    hi = x.astype(bf16)
    r1 = x - hi.astype(f32)
    mid = r1.astype(bf16)
    lo = (r1 - mid.astype(f32)).astype(bf16)
    return hi, mid, lo


def _group_sum(x, ones_bd):
    xb = x.astype(bf16)
    parts = []
    for g in range(x.shape[1] // MXU_DIM):
        cols = slice(g * MXU_DIM, (g + 1) * MXU_DIM)
        parts.append(_dot(xb[:, cols], ones_bd))
    return jnp.concatenate(parts, axis=1)


def _const_spec(shape):
    nd = len(shape)
    return pl.BlockSpec(shape, lambda *_: (0,) * nd, pipeline_mode=pl.Buffered(1))


def _inproj_body(x_ref, nw_ref, w_ref, qkw_ref, ones_ref, za_ref, q_ref, k_ref, v_ref):
    x = x_ref[...]
    ms = jnp.mean(x * x, axis=-1, keepdims=True)
    h = (x * lax.rsqrt(ms + NORM_EPS) * nw_ref[...]).astype(bf16)
    za_ref[...] = _dot(h, w_ref[:, :A_IN])
    ones_bd = ones_ref[...]

    def qk_norm(col0, row):
        z = _dot(h, w_ref[:, col0:col0 + B_WIDTH])
        ss = _group_sum(z * z, ones_bd)
        return (z * lax.rsqrt(ss * (1.0 / B_HEAD_DIM) + NORM_EPS) * qkw_ref[row:row + 1, :]).astype(bf16)

    q_ref[...] = qk_norm(A_IN, 0)
    k_ref[...] = qk_norm(A_IN + B_WIDTH, 1)
    v_ref[...] = _dot(h, w_ref[:, A_IN + 2 * B_WIDTH:]).astype(bf16)


def _inproj(x2d, norm_w, w_in_bf, qk_w, ones_bd):
    n = x2d.shape[0]
    tm = min(IN_TM, n)
    row = lambda i: (i, 0)
    return pl.pallas_call(
        _inproj_body,
        grid=(n // tm,),
        in_specs=[
            pl.BlockSpec((tm, D_MODEL), row),
            _const_spec((1, D_MODEL)),
            _const_spec((D_MODEL, IN_WIDTH)),
            _const_spec((2, B_WIDTH)),
            _const_spec((MXU_DIM, MXU_DIM)),
        ],
        out_specs=[
            pl.BlockSpec((tm, A_IN), row),
            pl.BlockSpec((tm, B_WIDTH), row),
            pl.BlockSpec((tm, B_WIDTH), row),
            pl.BlockSpec((tm, B_WIDTH), row),
        ],
        out_shape=[
            jax.ShapeDtypeStruct((n, A_IN), f32),
            jax.ShapeDtypeStruct((n, B_WIDTH), bf16),
            jax.ShapeDtypeStruct((n, B_WIDTH), bf16),
            jax.ShapeDtypeStruct((n, B_WIDTH), bf16),
        ],
        compiler_params=pltpu.CompilerParams(
            dimension_semantics=("parallel",), vmem_limit_bytes=VMEM_LIMIT),
        name="inproj",
    )(x2d, norm_w, w_in_bf, qk_w, ones_bd)


def _rwkv_body(za_ref, halo_ref, mu_ref, w0_ref, a0_ref, wlora_ref, wgate_ref, kk_ref, ka_ref,
               rk_ref, lnw_ref, lnb_ref, ones_ref, out_ref, s_ref):
    tb = pl.program_id(1)
    tc = za_ref.shape[1]
    n_chunks = tc // CHUNK

    @pl.when(tb == 0)
    def _():
        s_ref[...] = jnp.zeros_like(s_ref)

    h = za_ref[0]
    last = jnp.where(tb == 0, 0.0, halo_ref[0][7:8, :])
    row0 = lax.broadcasted_iota(jnp.int32, (tc, 1), 0) == 0
    prev = jnp.where(row0, last, pltpu.roll(h, 1, axis=0))
    hs = h + (prev - h) * mu_ref[...]

    r = hs[:, 0:A_WIDTH]
    k = hs[:, A_WIDTH:2 * A_WIDTH]
    v = hs[:, 2 * A_WIDTH:3 * A_WIDTH]
    lora_in = hs[:, 3 * A_WIDTH:3 * A_WIDTH + LANES]
    gate_in = hs[:, 3 * A_WIDTH + LANES:]
    lane = lax.broadcasted_iota(jnp.int32, (1, LANES), 1)
    lora_act = jnp.where(lane < DECAY_LORA, jnp.tanh(lora_in), lora_in).astype(bf16)
    lora = _dot(lora_act, wlora_ref[...])
    logw = -DECAY_SCALE * jax.nn.sigmoid(w0_ref[...] + lora[:, :A_WIDTH])
    lr = jax.nn.sigmoid(a0_ref[...] + lora[:, A_WIDTH:])
    gate = _dot(jax.nn.sigmoid(gate_in).astype(bf16), wgate_ref[...])

    ones_bd = ones_ref[...]
    kk = k * kk_ref[...]
    kk = kk * lax.rsqrt(_group_sum(kk * kk, ones_bd) + L2_EPS)
    kh = k * (1.0 + (lr - 1.0) * ka_ref[...])
    a_s = -kk
    b_s = kk * lr
    bonus = _group_sum(r * kh * rk_ref[...], ones_bd) * v

    ri = lax.broadcasted_iota(jnp.int32, (tc, tc), 0)
    ci = lax.broadcasted_iota(jnp.int32, (tc, tc), 1)
    in_chunk = (ri & (CHUNK - 1)).astype(jnp.uint32)
    dist = (ri - ci).astype(jnp.uint32)
    incl = dist <= in_chunk
    strict = (dist - 1) < in_chunk
    tri = jnp.where(incl, 1.0, 0.0).astype(bf16)
    w_hi, w_lo = _split2(logw)
    cs = _dot(tri, w_hi) + _dot(tri, w_lo)
    cs3 = cs.reshape(n_chunks, CHUNK, A_WIDTH)
    cs_end = jnp.broadcast_to(cs3[:, CHUNK - 1:CHUNK, :], cs3.shape).reshape(tc, A_WIDTH)

    e_neg = jnp.exp(-cs)
    e_end = jnp.exp(cs_end - cs)
    a_t = a_s * jnp.exp(cs - logw)
    r_t = r * jnp.exp(cs)
    b_t = b_s * e_neg
    k_t = kh * e_neg
    b_h = b_s * e_end
    k_h = kh * e_end
    g_end = jnp.exp(cs_end)

    heads = []
    for hd in range(A_HEADS):
        sl = slice((hd // 2) * LANES, (hd // 2 + 1) * LANES)
        lo = (hd % 2) * A_HEAD_DIM
        hm = (lane >= lo) & (lane < lo + A_HEAD_DIM)
        a_m = jnp.where(hm, a_t[:, sl], 0.0)
        r_m = jnp.where(hm, r_t[:, sl], 0.0)
        v_mb = jnp.where(hm, v[:, sl], 0.0).astype(bf16)
        heads.append(dict(sl=sl, a_m=a_m, r_m=r_m, v_mb=v_mb))
    for hd, st in enumerate(heads):
        sl = st["sl"]
        bk = jnp.concatenate([b_t[:, sl], k_t[:, sl]], axis=0).astype(bf16)
        ar = jnp.concatenate([st["a_m"], st["r_m"]], axis=0).astype(bf16)
        sc = _dot_nt(ar, bk)
        st["l_ab"] = jnp.where(strict, sc[:tc, :tc], 0.0).astype(bf16)
        st["l_ak"] = jnp.where(strict, sc[:tc, tc:], 0.0).astype(bf16)
        st["m_rb"] = jnp.where(incl, sc[tc:, :tc], 0.0).astype(bf16)
        st["m_rk"] = jnp.where(incl, sc[tc:, tc:], 0.0).astype(bf16)
    for st in heads:
        st["x"] = jnp.concatenate([st["a_m"], _dot(st["l_ak"], st["v_mb"])], axis=1)
    for st in heads:
        st["p"] = st["l_ab"]
        st["x"] = st["x"] + _dot(st["p"], st["x"].astype(bf16))
    for _ in range(5):
        for st in heads:
            st["p"] = _dot(st["p"], st["p"]).astype(bf16)
        for st in heads:
            st["x"] = st["x"] + _dot(st["p"], st["x"].astype(bf16))
    for st in heads:
        st["w"] = st["x"][:, :LANES]
        st["u"] = st["x"][:, LANES:]
        rb_wu = _dot(st["m_rb"], st["x"].astype(bf16))
        st["rp"] = st["r_m"] + rb_wu[:, :LANES]
        st["yi"] = rb_wu[:, LANES:] + _dot(st["m_rk"], st["v_mb"])

    qi = lax.broadcasted_iota(jnp.int32, (MXU_DIM, MXU_DIM), 0)
    qj = lax.broadcasted_iota(jnp.int32, (MXU_DIM, MXU_DIM), 1)
    same_head = (qi ^ qj) < A_HEAD_DIM
    n_groups = A_WIDTH // MXU_DIM
    per_group = A_HEADS // n_groups

    def group_cat(key, g):
        hs_g = heads[g * per_group:(g + 1) * per_group]
        pairs = [hs_g[i][key] + hs_g[i + 1][key] for i in range(0, per_group, 2)]
        return jnp.concatenate(pairs, axis=1)

    groups = []
    for g in range(n_groups):
        gs = slice(g * MXU_DIM, (g + 1) * MXU_DIM)
        groups.append(dict(gs=gs, w=group_cat("w", g), u=group_cat("u", g), rp=group_cat("rp", g),
                           yi=group_cat("yi", g), pt=[], qt=[], y=[]))
    for c in range(n_chunks):
        rows = slice(c * CHUNK, (c + 1) * CHUNK)
        for gr in groups:
            gs = gr["gs"]
            bh_c = b_h[rows, gs]
            gr["pt"].append(jnp.where(same_head, _dot(gr["w"][rows].T.astype(bf16), bh_c.astype(bf16)),
                                      0.0).astype(bf16))
            uv = jnp.concatenate([gr["u"][rows], v[rows, gs]], axis=0)
            bkh = jnp.concatenate([bh_c, k_h[rows, gs]], axis=0)
            gr["qt"].append(jnp.where(same_head, _dot(uv.T.astype(bf16), bkh.astype(bf16)), 0.0))
    states = [s_ref[g] for g in range(n_groups)]
    for c in range(n_chunks):
        rows = slice(c * CHUNK, (c + 1) * CHUNK)
        for g, gr in enumerate(groups):
            s_b = states[g].astype(bf16)
            gr["y"].append(_dot_nt(gr["rp"][rows].astype(bf16), s_b) + gr["yi"][rows])
            states[g] = (states[g] * g_end[c * CHUNK:c * CHUNK + 1, gr["gs"]]
                         + _dot(s_b, gr["pt"][c]) + gr["qt"][c])
    for g in range(n_groups):
        s_ref[g] = states[g]
    y = jnp.concatenate([jnp.concatenate(gr["y"], axis=0) for gr in groups], axis=1)

    inv_n = 1.0 / A_HEAD_DIM
    mean = _group_sum(y, ones_bd) * inv_n
    d = y - mean
    var = _group_sum(d * d, ones_bd) * inv_n
    yn = d * lax.rsqrt(var + LNX_EPS) * lnw_ref[...] + lnb_ref[...]
    out_ref[0] = ((yn + bonus) * gate).astype(out_ref.dtype)


def _rwkv(za, mu, w0, a0, w_lora, w_gate, k_k, k_a, r_k, ln_w, ln_b, ones_bd):
    bsz, t, _ = za.shape
    tc = min(SCAN_T, t)
    halo_rows = 8
    per = tc // halo_rows
    vec = lambda a: a.reshape(1, -1).astype(f32)
    return pl.pallas_call(
        _rwkv_body,
        grid=(bsz, t // tc),
        in_specs=[
            pl.BlockSpec((1, tc, A_IN), lambda b, i: (b, i, 0)),
            pl.BlockSpec((1, halo_rows, A_IN), lambda b, i: (b, jnp.maximum(i * per - 1, 0), 0)),
            _const_spec((1, A_IN)),
            _const_spec((1, A_WIDTH)),
            _const_spec((1, A_WIDTH)),
            _const_spec((LANES, 2 * A_WIDTH)),
            _const_spec((GATE_LORA, A_WIDTH)),
            _const_spec((1, A_WIDTH)),
            _const_spec((1, A_WIDTH)),
            _const_spec((1, A_WIDTH)),
            _const_spec((1, A_WIDTH)),
            _const_spec((1, A_WIDTH)),
            _const_spec((MXU_DIM, MXU_DIM)),
        ],
        out_specs=pl.BlockSpec((1, tc, A_WIDTH), lambda b, i: (b, i, 0)),
        out_shape=jax.ShapeDtypeStruct((bsz, t, A_WIDTH), bf16),
        scratch_shapes=[pltpu.VMEM((A_WIDTH // MXU_DIM, MXU_DIM, MXU_DIM), f32)],
        compiler_params=pltpu.CompilerParams(
            dimension_semantics=("parallel", "arbitrary"), vmem_limit_bytes=VMEM_LIMIT),
        name="rwkv7",
    )(za, za, vec(mu), vec(w0), vec(a0), w_lora, w_gate, vec(k_k), vec(k_a), vec(r_k),
      vec(ln_w), vec(ln_b), ones_bd)


def _attn_body(bias_ref, q_ref, k_ref, v_ref, diag_ref, lam_ref, sub_ref, o_ref,
               ka_ref, vt_ref, qs_ref, st_ref, mx_ref, m_ref, acc_ref, *, one_minus_lambda_init, lambda_init):
    hd = pl.program_id(1)
    phase = pl.program_id(2)
    t = k_ref.shape[0]
    bq = diag_ref.shape[1]
    hw = 2 * B_HEAD_DIM
    n_bias = bias_ref.shape[1]
    lane = lax.broadcasted_iota(jnp.int32, (1, hw), 1)

    @pl.when(phase == 0)
    def _():
        pos = lax.broadcasted_iota(jnp.int32, (t, hw), 0)
        col = lax.broadcasted_iota(jnp.int32, (t, hw), 1)
        pos_cols = jnp.where(col < n_bias, jnp.where((col & 1) == 0, pos >> 8, pos & 255), 0)
        ka_ref[:, :hw] = k_ref[...]
        ka_ref[:, hw:] = pos_cols.astype(f32).astype(bf16)
        vt_ref[:hw, :] = v_ref[...].astype(f32).T.astype(bf16)
        ones_row = lax.broadcasted_iota(jnp.int32, (ATT_ONES_ROWS, t), 0) == 0
        vt_ref[hw:, :] = jnp.where(ones_row, 1.0, 0.0).astype(bf16)
        q_pos = jnp.zeros((1, hw), f32)
        for i in range(n_bias):
            q_pos = jnp.where(lane == i, bias_ref[hd, i], q_pos)
        q_pos = jnp.broadcast_to(q_pos, (bq, hw)).astype(bf16)
        for i in range(t // bq):
            q = q_ref[i * bq:(i + 1) * bq, :]
            zero = jnp.zeros_like(q)
            qs_ref[i, :bq, :hw] = jnp.where(lane < B_HEAD_DIM, q, zero)
            qs_ref[i, bq:, :hw] = jnp.where(lane >= B_HEAD_DIM, q, zero)
            qs_ref[i, :bq, hw:] = q_pos
            qs_ref[i, bq:, hw:] = q_pos

    @pl.when(phase == 1)
    def _():
        _attn_tiles(q_ref, diag_ref, lam_ref, sub_ref, o_ref, ka_ref, vt_ref, qs_ref, st_ref, mx_ref,
                    m_ref, acc_ref, one_minus_lambda_init, lambda_init)


def _attn_tiles(q_ref, diag_ref, lam_ref, sub_ref, o_ref, ka_ref, vt_ref, qs_ref, st_ref, mx_ref,
                m_ref, acc_ref, one_minus_lambda_init, lambda_init):
    t = ka_ref.shape[0]
    bq = diag_ref.shape[1]
    hw = 2 * B_HEAD_DIM

    lq1, lk1, lq2, lk2 = lam_ref[0:1, :], lam_ref[1:2, :], lam_ref[2:3, :], lam_ref[3:4, :]
    lam = (jnp.exp(jnp.sum(lq1 * lk1, axis=-1, keepdims=True))
           - jnp.exp(jnp.sum(lq2 * lk2, axis=-1, keepdims=True)) + lambda_init)

    def scores(i, j, slot):
        st = _dot_nt(ka_ref[j * bq:(j + 1) * bq, :], qs_ref[i])
        if j == i:
            st = st + diag_ref[0]
        st_ref[slot] = st
        mx_ref[slot] = jnp.max(st, axis=0, keepdims=True)

    def update(j, slot):
        m_old = m_ref[...]
        m_new = jnp.maximum(m_old, mx_ref[slot])
        alpha = jnp.exp2(m_old - m_new)
        p = jnp.exp2(st_ref[slot] - m_new).astype(bf16)
        acc_ref[...] = alpha * acc_ref[...] + _dot(vt_ref[:, j * bq:(j + 1) * bq], p)
        m_ref[...] = m_new

    def finalize(i):
        ot = acc_ref[:hw, :] * (1.0 / acc_ref[hw:hw + 1, :])
        o = (ot[:, :bq] - lam * ot[:, bq:]).T
        ms = jnp.mean(o * o, axis=-1, keepdims=True)
        o = o * lax.rsqrt(ms + NORM_EPS) * sub_ref[...] * one_minus_lambda_init
        o_ref[i * bq:(i + 1) * bq, :] = o.astype(o_ref.dtype)

    tiles = [(i, j) for i in range(t // bq) for j in range(i + 1)]
    scores(*tiles[0], 0)
    for n, (i, j) in enumerate(tiles):
        slot = n % 2
        if n + 1 < len(tiles):
            scores(*tiles[n + 1], 1 - slot)
        if j == 0:
            m_ref[...] = jnp.full_like(m_ref, NEG_BIG)
            acc_ref[...] = jnp.zeros_like(acc_ref)
        update(j, slot)
        if j == i:
            finalize(i)


def _attention(q, k, v, bsz, t, lam_vecs, subln_w, lambda_init):
    bq = min(ATT_BQ, t)
    nq = t // bq
    assert t <= 256 * 256, "key positions are split into two bf16-exact columns"
    slopes = jnp.exp2(-8.0 * jnp.arange(1, B_HEADS + 1, dtype=f32) / B_HEADS) * LOG2_E
    c_hi, c_mid, c_lo = (piece.astype(f32) for piece in _split3(slopes))
    bias_cols = jnp.stack([256.0 * c_hi, c_hi, 256.0 * c_mid, c_mid, 256.0 * c_lo, c_lo], axis=1)
    kk = jnp.arange(bq)[:, None]
    qq = jnp.arange(bq)[None, :]
    later = jnp.where(kk > qq, 2.0 * (qq - kk), 0.0).astype(f32)
    allowed = (kk // CHUNK) <= (qq // CHUNK)
    diag = jnp.where(allowed[None], slopes[:, None, None] * later[None], NEG_BIG)
    diag = jnp.concatenate([diag, diag], axis=2)
    hw = 2 * B_HEAD_DIM
    body = functools.partial(_attn_body, one_minus_lambda_init=1.0 - lambda_init, lambda_init=lambda_init)
    head_block = lambda: pl.BlockSpec((t, hw), lambda b, h, p: (b, h))
    return pl.pallas_call(
        body,
        grid=(bsz, B_HEADS, 2),
        in_specs=[
            pl.BlockSpec(memory_space=pltpu.SMEM),
            head_block(),
            head_block(),
            head_block(),
            pl.BlockSpec((1, bq, 2 * bq), lambda b, h, p: (h, 0, 0)),
            pl.BlockSpec((4, B_HEAD_DIM), lambda b, h, p: (0, 0)),
            pl.BlockSpec((1, hw), lambda b, h, p: (0, 0)),
        ],
        out_specs=head_block(),
        out_shape=jax.ShapeDtypeStruct((bsz * t, B_WIDTH), bf16),
        scratch_shapes=[
            pltpu.VMEM((t, 2 * hw), bf16),
            pltpu.VMEM((hw + ATT_ONES_ROWS, t), bf16),
            pltpu.VMEM((nq, 2 * bq, 2 * hw), bf16),
            pltpu.VMEM((2, bq, 2 * bq), f32),
            pltpu.VMEM((2, 1, 2 * bq), f32),
            pltpu.VMEM((1, 2 * bq), f32),
            pltpu.VMEM((hw + ATT_ONES_ROWS, 2 * bq), f32),
        ],
        compiler_params=pltpu.CompilerParams(
            dimension_semantics=("parallel", "parallel", "arbitrary"), vmem_limit_bytes=VMEM_LIMIT),
        name="diff_attn",
    )(bias_cols, q, k, v, diag, lam_vecs, subln_w.reshape(1, hw).astype(f32))


def _ffn_body(x_ref, ya_ref, yb_ref, wo_ref, nw_ref, wup_ref, cw_ref, cb_ref, wdn_ref, o_ref,
              carry_ref, act_ref, h_ref):
    tb = pl.program_id(1)
    tm = x_ref.shape[1]
    x2 = x_ref[0] + _dot(ya_ref[0], wo_ref[:A_WIDTH, :]) + _dot(yb_ref[0], wo_ref[A_WIDTH:, :])
    ms = jnp.mean(x2 * x2, axis=-1, keepdims=True)
    h_ref[...] = (x2 * lax.rsqrt(ms + NORM_EPS) * nw_ref[...]).astype(bf16)

    @pl.when(tb == 0)
    def _():
        carry_ref[...] = jnp.zeros_like(carry_ref)

    def conv(u, col):
        cols = slice(col, col + FFN_TF)
        old = carry_ref[:, cols]
        carry_ref[:, cols] = u[tm - 8:, :]
        row8 = lax.broadcasted_iota(jnp.int32, (8, 1), 0)
        u1 = pltpu.roll(u, 1, axis=0)
        u2 = pltpu.roll(u, 2, axis=0)
        u1 = jnp.concatenate([jnp.where(row8 < 1, pltpu.roll(old, 1, axis=0), u1[:8]), u1[8:]], axis=0)
        u2 = jnp.concatenate([jnp.where(row8 < 2, pltpu.roll(old, 2, axis=0), u2[:8]), u2[8:]], axis=0)
        return (cw_ref[0:1, cols] * u2 + cw_ref[1:2, cols] * u1
                + cw_ref[2:3, cols] * u + cb_ref[:, cols])

    for j in range(D_FF // FFN_TF):
        gcol = j * FFN_TF
        ucol = D_FF + j * FFN_TF
        gate = conv(_dot(h_ref[...], wup_ref[:, gcol:gcol + FFN_TF]), gcol)
        up = conv(_dot(h_ref[...], wup_ref[:, ucol:ucol + FFN_TF]), ucol)
        act_ref[:, gcol:gcol + FFN_TF] = (gate * jax.nn.sigmoid(gate) * up).astype(bf16)
    o_ref[0] = x2 + _dot(act_ref[...], wdn_ref[...])


def _out_ffn(x, ya, yb, w_out_bf, norm_w, w_up_bf, conv_w, conv_b, w_down_bf):
    bsz, t, _ = x.shape
    tm = min(FFN_TM, t)
    tok = lambda b, i: (b, i, 0)
    return pl.pallas_call(
        _ffn_body,
        grid=(bsz, t // tm),
        in_specs=[
            pl.BlockSpec((1, tm, D_MODEL), tok),
            pl.BlockSpec((1, tm, A_WIDTH), tok),
            pl.BlockSpec((1, tm, B_WIDTH), tok),
            _const_spec((A_WIDTH + B_WIDTH, D_MODEL)),
            _const_spec((1, D_MODEL)),
            _const_spec((D_MODEL, 2 * D_FF)),
            _const_spec((3, 2 * D_FF)),
            _const_spec((1, 2 * D_FF)),
            _const_spec((D_FF, D_MODEL)),
        ],
        out_specs=pl.BlockSpec((1, tm, D_MODEL), tok),
        out_shape=jax.ShapeDtypeStruct((bsz, t, D_MODEL), f32),
        scratch_shapes=[
            pltpu.VMEM((8, 2 * D_FF), f32),
            pltpu.VMEM((tm, D_FF), bf16),
            pltpu.VMEM((tm, D_MODEL), bf16),
        ],
        compiler_params=pltpu.CompilerParams(
            dimension_semantics=("parallel", "arbitrary"), vmem_limit_bytes=VMEM_LIMIT),
        name="out_ffn",
    )(x, ya, yb, w_out_bf, norm_w, w_up_bf, conv_w, conv_b, w_down_bf)


def _layer(x, lambda_init, attn_norm_w, w_in, mu_shift, w0, w_decay_up, a0, w_aaa_up, w_gate_up,
           k_k, k_a, r_k, ln_x_w, ln_x_b, q_norm_w, k_norm_w, lambda_q1, lambda_k1, lambda_q2,
           lambda_k2, subln_w, w_out, ffn_norm_w, w_ffn_up, ffn_conv_w, ffn_conv_b, w_ffn_down):
    bsz, t, _ = x.shape
    head_ids = jnp.arange(MXU_DIM) // A_HEAD_DIM
    ones_bd = (head_ids[:, None] == head_ids[None, :]).astype(bf16)
    scale = LOG2_E / math.sqrt(B_HEAD_DIM)
    qk_w = jnp.stack([jnp.tile(q_norm_w.astype(f32), B_WIDTH // B_HEAD_DIM) * scale,
                      jnp.tile(k_norm_w.astype(f32), B_WIDTH // B_HEAD_DIM)])
    za, q, k, v = _inproj(x.reshape(bsz * t, D_MODEL), attn_norm_w.reshape(1, D_MODEL),
                          w_in.astype(bf16), qk_w, ones_bd)

    zeros = jnp.zeros((DECAY_LORA, A_WIDTH), f32)
    w_lora = jnp.concatenate([jnp.concatenate([w_decay_up, zeros], axis=1),
                              jnp.concatenate([zeros, w_aaa_up], axis=1)], axis=0).astype(bf16)
    ya = _rwkv(za.reshape(bsz, t, A_IN), mu_shift, w0, a0, w_lora, w_gate_up.astype(bf16),
               k_k, k_a, r_k, ln_x_w, ln_x_b, ones_bd)

    lam_vecs = jnp.stack([lambda_q1, lambda_k1, lambda_q2, lambda_k2]).astype(f32)
    yb = _attention(q, k, v, bsz, t, lam_vecs, subln_w, lambda_init)

    return _out_ffn(x, ya, yb.reshape(bsz, t, B_WIDTH), w_out.astype(bf16),
                    ffn_norm_w.reshape(1, D_MODEL), w_ffn_up.astype(bf16), ffn_conv_w,
                    ffn_conv_b.reshape(1, 2 * D_FF), w_ffn_down.astype(bf16))


def kernel(x, attn_norm_w, w_in, mu_shift, w0, w_decay_up, a0, w_aaa_up, w_gate_up, k_k, k_a, r_k,
           ln_x_w, ln_x_b, q_norm_w, k_norm_w, lambda_q1, lambda_k1, lambda_q2, lambda_k2, subln_w,
           w_out, ffn_norm_w, w_ffn_up, ffn_conv_w, ffn_conv_b, w_ffn_down):
    params = (attn_norm_w, w_in, mu_shift, w0, w_decay_up, a0, w_aaa_up, w_gate_up, k_k, k_a, r_k,
              ln_x_w, ln_x_b, q_norm_w, k_norm_w, lambda_q1, lambda_k1, lambda_q2, lambda_k2,
              subln_w, w_out, ffn_norm_w, w_ffn_up, ffn_conv_w, ffn_conv_b, w_ffn_down)
    for l in range(attn_norm_w.shape[0]):
        lambda_init = 0.8 - 0.6 * math.exp(-0.3 * l)
        x = _layer(x, lambda_init, *(p[l] for p in params))
    return x
```

```python
import functools
import math

import jax
import jax.numpy as jnp
from jax import lax
from jax.experimental import pallas as pl
from jax.experimental.pallas import tpu as pltpu

f32 = jnp.float32
bf16 = jnp.bfloat16

D_MODEL = 1024
CHUNK = 64
A_HEADS = 8
A_HEAD_DIM = 64
A_WIDTH = A_HEADS * A_HEAD_DIM
DECAY_LORA = 64
AAA_LORA = 64
GATE_LORA = 128
A_IN = 3 * A_WIDTH + DECAY_LORA + AAA_LORA + GATE_LORA
B_HEADS = 4
B_HEAD_DIM = 64
B_WIDTH = B_HEADS * 2 * B_HEAD_DIM
B_IN = 3 * B_WIDTH
IN_WIDTH = A_IN + B_IN
D_FF = 2816
NORM_EPS = 1e-6
LNX_EPS = 64e-5
DECAY_SCALE = math.exp(-0.5)
L2_EPS = 1e-12

LANES = 128
MXU_DIM = 256
VMEM_LIMIT = 56 * 1024 * 1024

IN_TM = 512
SCAN_T = 256
ATT_BQ = 512
ATT_ONES_ROWS = 16
LOG2_E = math.log2(math.e)
FFN_TM = 512
FFN_TF = 256
NEG_BIG = -1e30


def _dot(a, b, dims=((1,), (0,))):
    return lax.dot_general(a, b, (dims, ((), ())), preferred_element_type=f32)


def _dot_nt(a, b):
    return _dot(a, b, ((1,), (1,)))


def _split2(x):
    hi = x.astype(bf16)
    lo = (x - hi.astype(f32)).astype(bf16)
    return hi, lo


def _split3(x):
    hi = x.astype(bf16)
    r1 = x - hi.astype(f32)
    mid = r1.astype(bf16)
    lo = (r1 - mid.astype(f32)).astype(bf16)
    return hi, mid, lo


def _group_sum(x, ones_bd):
    xb = x.astype(bf16)
    parts = []
    for g in range(x.shape[1] // MXU_DIM):
        cols = slice(g * MXU_DIM, (g + 1) * MXU_DIM)
        parts.append(_dot(xb[:, cols], ones_bd))
    return jnp.concatenate(parts, axis=1)


def _const_spec(shape):
    nd = len(shape)
    return pl.BlockSpec(shape, lambda *_: (0,) * nd, pipeline_mode=pl.Buffered(1))


def _inproj_body(x_ref, nw_ref, w_ref, qkw_ref, ones_ref, za_ref, q_ref, k_ref, v_ref):
    x = x_ref[...]
    ms = jnp.mean(x * x, axis=-1, keepdims=True)
    h = (x * lax.rsqrt(ms + NORM_EPS) * nw_ref[...]).astype(bf16)
    za_ref[...] = _dot(h, w_ref[:, :A_IN])
    ones_bd = ones_ref[...]

    def qk_norm(col0, row):
        z = _dot(h, w_ref[:, col0:col0 + B_WIDTH])
        ss = _group_sum(z * z, ones_bd)
        return (z * lax.rsqrt(ss * (1.0 / B_HEAD_DIM) + NORM_EPS) * qkw_ref[row:row + 1, :]).astype(bf16)

    q_ref[...] = qk_norm(A_IN, 0)
    k_ref[...] = qk_norm(A_IN + B_WIDTH, 1)
    v_ref[...] = _dot(h, w_ref[:, A_IN + 2 * B_WIDTH:]).astype(bf16)


def _inproj(x2d, norm_w, w_in_bf, qk_w, ones_bd):
    n = x2d.shape[0]
    tm = min(IN_TM, n)
    row = lambda i: (i, 0)
    return pl.pallas_call(
        _inproj_body,
        grid=(n // tm,),
        in_specs=[
            pl.BlockSpec((tm, D_MODEL), row),
            _const_spec((1, D_MODEL)),
            _const_spec((D_MODEL, IN_WIDTH)),
            _const_spec((2, B_WIDTH)),
            _const_spec((MXU_DIM, MXU_DIM)),
        ],
        out_specs=[
            pl.BlockSpec((tm, A_IN), row),
            pl.BlockSpec((tm, B_WIDTH), row),
            pl.BlockSpec((tm, B_WIDTH), row),
            pl.BlockSpec((tm, B_WIDTH), row),
        ],
        out_shape=[
            jax.ShapeDtypeStruct((n, A_IN), f32),
            jax.ShapeDtypeStruct((n, B_WIDTH), bf16),
            jax.ShapeDtypeStruct((n, B_WIDTH), bf16),
            jax.ShapeDtypeStruct((n, B_WIDTH), bf16),
        ],
        compiler_params=pltpu.CompilerParams(
            dimension_semantics=("parallel",), vmem_limit_bytes=VMEM_LIMIT),
        name="inproj",
    )(x2d, norm_w, w_in_bf, qk_w, ones_bd)


def _rwkv_body(za_ref, halo_ref, mu_ref, w0_ref, a0_ref, wlora_ref, wgate_ref, kk_ref, ka_ref,
               rk_ref, lnw_ref, lnb_ref, ones_ref, out_ref, s_ref):
    tb = pl.program_id(1)
    tc = za_ref.shape[1]
    n_chunks = tc // CHUNK

    @pl.when(tb == 0)
    def _():
        s_ref[...] = jnp.zeros_like(s_ref)

    h = za_ref[0]
    last = jnp.where(tb == 0, 0.0, halo_ref[0][7:8, :])
    row0 = lax.broadcasted_iota(jnp.int32, (tc, 1), 0) == 0
    prev = jnp.where(row0, last, pltpu.roll(h, 1, axis=0))
    hs = h + (prev - h) * mu_ref[...]

    r = hs[:, 0:A_WIDTH]
    k = hs[:, A_WIDTH:2 * A_WIDTH]
    v = hs[:, 2 * A_WIDTH:3 * A_WIDTH]
    lora_in = hs[:, 3 * A_WIDTH:3 * A_WIDTH + LANES]
    gate_in = hs[:, 3 * A_WIDTH + LANES:]
    lane = lax.broadcasted_iota(jnp.int32, (1, LANES), 1)
    lora_act = jnp.where(lane < DECAY_LORA, jnp.tanh(lora_in), lora_in).astype(bf16)
    lora = _dot(lora_act, wlora_ref[...])
    logw = -DECAY_SCALE * jax.nn.sigmoid(w0_ref[...] + lora[:, :A_WIDTH])
    lr = jax.nn.sigmoid(a0_ref[...] + lora[:, A_WIDTH:])
    gate = _dot(jax.nn.sigmoid(gate_in).astype(bf16), wgate_ref[...])

    ones_bd = ones_ref[...]
    kk = k * kk_ref[...]
    kk = kk * lax.rsqrt(_group_sum(kk * kk, ones_bd) + L2_EPS)
    kh = k * (1.0 + (lr - 1.0) * ka_ref[...])
    a_s = -kk
    b_s = kk * lr
    bonus = _group_sum(r * kh * rk_ref[...], ones_bd) * v

    ri = lax.broadcasted_iota(jnp.int32, (tc, tc), 0)
    ci = lax.broadcasted_iota(jnp.int32, (tc, tc), 1)
    in_chunk = (ri & (CHUNK - 1)).astype(jnp.uint32)
    dist = (ri - ci).astype(jnp.uint32)
    incl = dist <= in_chunk
    strict = (dist - 1) < in_chunk
    tri = jnp.where(incl, 1.0, 0.0).astype(bf16)
    w_hi, w_lo = _split2(logw)
    cs = _dot(tri, w_hi) + _dot(tri, w_lo)
    cs3 = cs.reshape(n_chunks, CHUNK, A_WIDTH)
    cs_end = jnp.broadcast_to(cs3[:, CHUNK - 1:CHUNK, :], cs3.shape).reshape(tc, A_WIDTH)

    e_neg = jnp.exp(-cs)
    e_end = jnp.exp(cs_end - cs)
    a_t = a_s * jnp.exp(cs - logw)
    r_t = r * jnp.exp(cs)
    b_t = b_s * e_neg
    k_t = kh * e_neg
    b_h = b_s * e_end
    k_h = kh * e_end
    g_end = jnp.exp(cs_end)

    heads = []
    for hd in range(A_HEADS):
        sl = slice((hd // 2) * LANES, (hd // 2 + 1) * LANES)
        lo = (hd % 2) * A_HEAD_DIM
        hm = (lane >= lo) & (lane < lo + A_HEAD_DIM)
        a_m = jnp.where(hm, a_t[:, sl], 0.0)
        r_m = jnp.where(hm, r_t[:, sl], 0.0)
        v_mb = jnp.where(hm, v[:, sl], 0.0).astype(bf16)
        heads.append(dict(sl=sl, a_m=a_m, r_m=r_m, v_mb=v_mb))
    gi = lax.broadcasted_iota(jnp.int32, (MXU_DIM, MXU_DIM), 0)
    gj = lax.broadcasted_iota(jnp.int32, (MXU_DIM, MXU_DIM), 1)
    own_lanes = (gi ^ gj) < A_HEAD_DIM
    per_group = MXU_DIM // A_HEAD_DIM
    slabs = {}
    for g in range(A_WIDTH // MXU_DIM):
        gs = slice(g * MXU_DIM, (g + 1) * MXU_DIM)
        for c in range(n_chunks):
            rows = slice(c * CHUNK, (c + 1) * CHUNK)
            ar = jnp.concatenate([a_t[rows, gs], r_t[rows, gs]], axis=0).astype(bf16)
            b4 = jnp.where(own_lanes, jnp.concatenate([b_t[rows, gs]] * per_group, axis=0), 0.0)
            k4 = jnp.where(own_lanes, jnp.concatenate([k_t[rows, gs]] * per_group, axis=0), 0.0)
            slabs[g, c] = _dot_nt(ar, jnp.concatenate([b4, k4], axis=0).astype(bf16))
    for hd, st in enumerate(heads):
        g, hl = divmod(hd, per_group)

        def block_diag(row0, col0):
            parts = []
            for c in range(n_chunks):
                slab = slabs[g, c][row0:row0 + CHUNK, col0:col0 + MXU_DIM]
                shift = ((c - hl) * A_HEAD_DIM) % MXU_DIM
                parts.append(pltpu.roll(slab, shift, axis=1) if shift else slab)
            return jnp.concatenate(parts, axis=0)

        st["l_ab"] = jnp.where(strict, block_diag(0, 0), 0.0).astype(bf16)
        st["l_ak"] = jnp.where(strict, block_diag(0, MXU_DIM), 0.0).astype(bf16)
        st["m_rb"] = jnp.where(incl, block_diag(CHUNK, 0), 0.0).astype(bf16)
        st["m_rk"] = jnp.where(incl, block_diag(CHUNK, MXU_DIM), 0.0).astype(bf16)
    for st in heads:
        st["x"] = jnp.concatenate([st["a_m"], _dot(st["l_ak"], st["v_mb"])], axis=1)
    for st in heads:
        st["p"] = st["l_ab"]
        st["x"] = st["x"] + _dot(st["p"], st["x"].astype(bf16))
    for _ in range(5):
        for st in heads:
            st["p"] = _dot(st["p"], st["p"]).astype(bf16)
        for st in heads:
            st["x"] = st["x"] + _dot(st["p"], st["x"].astype(bf16))
    for st in heads:
        st["w"] = st["x"][:, :LANES]
        st["u"] = st["x"][:, LANES:]
        rb_wu = _dot(st["m_rb"], st["x"].astype(bf16))
        st["rp"] = st["r_m"] + rb_wu[:, :LANES]
        st["yi"] = rb_wu[:, LANES:] + _dot(st["m_rk"], st["v_mb"])

    qi = lax.broadcasted_iota(jnp.int32, (MXU_DIM, MXU_DIM), 0)
    qj = lax.broadcasted_iota(jnp.int32, (MXU_DIM, MXU_DIM), 1)
    same_head = (qi ^ qj) < A_HEAD_DIM
    n_groups = A_WIDTH // MXU_DIM
    per_group = A_HEADS // n_groups

    def group_cat(key, g):
        hs_g = heads[g * per_group:(g + 1) * per_group]
        pairs = [hs_g[i][key] + hs_g[i + 1][key] for i in range(0, per_group, 2)]
        return jnp.concatenate(pairs, axis=1)

    groups = []
    for g in range(n_groups):
        gs = slice(g * MXU_DIM, (g + 1) * MXU_DIM)
        groups.append(dict(gs=gs, w=group_cat("w", g), u=group_cat("u", g), rp=group_cat("rp", g),
                           yi=group_cat("yi", g), pt=[], qt=[], y=[]))
    for c in range(n_chunks):
        rows = slice(c * CHUNK, (c + 1) * CHUNK)
        for gr in groups:
            gs = gr["gs"]
            bh_c = b_h[rows, gs]
            gr["pt"].append(jnp.where(same_head, _dot(gr["w"][rows].T.astype(bf16), bh_c.astype(bf16)),
                                      0.0).astype(bf16))
            uv = jnp.concatenate([gr["u"][rows], v[rows, gs]], axis=0)
            bkh = jnp.concatenate([bh_c, k_h[rows, gs]], axis=0)
            gr["qt"].append(jnp.where(same_head, _dot(uv.T.astype(bf16), bkh.astype(bf16)), 0.0))
    states = [s_ref[g] for g in range(n_groups)]
    for c in range(n_chunks):
        rows = slice(c * CHUNK, (c + 1) * CHUNK)
        for g, gr in enumerate(groups):
            s_b = states[g].astype(bf16)
            gr["y"].append(_dot_nt(gr["rp"][rows].astype(bf16), s_b) + gr["yi"][rows])
            states[g] = (states[g] * g_end[c * CHUNK:c * CHUNK + 1, gr["gs"]]
                         + _dot(s_b, gr["pt"][c]) + gr["qt"][c])
    for g in range(n_groups):
        s_ref[g] = states[g]
    y = jnp.concatenate([jnp.concatenate(gr["y"], axis=0) for gr in groups], axis=1)

    inv_n = 1.0 / A_HEAD_DIM
    mean = _group_sum(y, ones_bd) * inv_n
    d = y - mean
    var = _group_sum(d * d, ones_bd) * inv_n
    yn = d * lax.rsqrt(var + LNX_EPS) * lnw_ref[...] + lnb_ref[...]
    out_ref[0] = ((yn + bonus) * gate).astype(out_ref.dtype)


def _rwkv(za, mu, w0, a0, w_lora, w_gate, k_k, k_a, r_k, ln_w, ln_b, ones_bd):
    bsz, t, _ = za.shape
    tc = min(SCAN_T, t)
    halo_rows = 8
    per = tc // halo_rows
    vec = lambda a: a.reshape(1, -1).astype(f32)
    return pl.pallas_call(
        _rwkv_body,
        grid=(bsz, t // tc),
        in_specs=[
            pl.BlockSpec((1, tc, A_IN), lambda b, i: (b, i, 0)),
            pl.BlockSpec((1, halo_rows, A_IN), lambda b, i: (b, jnp.maximum(i * per - 1, 0), 0)),
            _const_spec((1, A_IN)),
            _const_spec((1, A_WIDTH)),
            _const_spec((1, A_WIDTH)),
            _const_spec((LANES, 2 * A_WIDTH)),
            _const_spec((GATE_LORA, A_WIDTH)),
            _const_spec((1, A_WIDTH)),
            _const_spec((1, A_WIDTH)),
            _const_spec((1, A_WIDTH)),
            _const_spec((1, A_WIDTH)),
            _const_spec((1, A_WIDTH)),
            _const_spec((MXU_DIM, MXU_DIM)),
        ],
        out_specs=pl.BlockSpec((1, tc, A_WIDTH), lambda b, i: (b, i, 0)),
        out_shape=jax.ShapeDtypeStruct((bsz, t, A_WIDTH), bf16),
        scratch_shapes=[pltpu.VMEM((A_WIDTH // MXU_DIM, MXU_DIM, MXU_DIM), f32)],
        compiler_params=pltpu.CompilerParams(
            dimension_semantics=("parallel", "arbitrary"), vmem_limit_bytes=VMEM_LIMIT),
        name="rwkv7",
    )(za, za, vec(mu), vec(w0), vec(a0), w_lora, w_gate, vec(k_k), vec(k_a), vec(r_k),
      vec(ln_w), vec(ln_b), ones_bd)


def _attn_body(bias_ref, q_ref, k_ref, v_ref, diag_ref, lam_ref, sub_ref, o_ref,
               ka_ref, vt_ref, qs_ref, st_ref, mx_ref, m_ref, acc_ref, *, one_minus_lambda_init, lambda_init):
    hd = pl.program_id(1)
    phase = pl.program_id(2)
    t = k_ref.shape[0]
    bq = diag_ref.shape[1]
    hw = 2 * B_HEAD_DIM
    n_bias = bias_ref.shape[1]
    lane = lax.broadcasted_iota(jnp.int32, (1, hw), 1)

    @pl.when(phase == 0)
    def _():
        pos = lax.broadcasted_iota(jnp.int32, (t, hw), 0)
        col = lax.broadcasted_iota(jnp.int32, (t, hw), 1)
        pos_cols = jnp.where(col < n_bias, jnp.where((col & 1) == 0, pos >> 8, pos & 255), 0)
        ka_ref[:, :hw] = k_ref[...]
        ka_ref[:, hw:] = pos_cols.astype(f32).astype(bf16)
        vt_ref[:hw, :] = v_ref[...].astype(f32).T.astype(bf16)
        ones_row = lax.broadcasted_iota(jnp.int32, (ATT_ONES_ROWS, t), 0) == 0
        vt_ref[hw:, :] = jnp.where(ones_row, 1.0, 0.0).astype(bf16)
        q_pos = jnp.zeros((1, hw), f32)
        for i in range(n_bias):
            q_pos = jnp.where(lane == i, bias_ref[hd, i], q_pos)
        q_pos = jnp.broadcast_to(q_pos, (bq, hw)).astype(bf16)
        for i in range(t // bq):
            q = q_ref[i * bq:(i + 1) * bq, :]
            zero = jnp.zeros_like(q)
            qs_ref[i, :bq, :hw] = jnp.where(lane < B_HEAD_DIM, q, zero)
            qs_ref[i, bq:, :hw] = jnp.where(lane >= B_HEAD_DIM, q, zero)
            qs_ref[i, :bq, hw:] = q_pos
            qs_ref[i, bq:, hw:] = q_pos

    @pl.when(phase == 1)
    def _():
        _attn_tiles(q_ref, diag_ref, lam_ref, sub_ref, o_ref, ka_ref, vt_ref, qs_ref, st_ref, mx_ref,
                    m_ref, acc_ref, one_minus_lambda_init, lambda_init)


def _attn_tiles(q_ref, diag_ref, lam_ref, sub_ref, o_ref, ka_ref, vt_ref, qs_ref, st_ref, mx_ref,
                m_ref, acc_ref, one_minus_lambda_init, lambda_init):
    t = ka_ref.shape[0]
    bq = diag_ref.shape[1]
    hw = 2 * B_HEAD_DIM

    lq1, lk1, lq2, lk2 = lam_ref[0:1, :], lam_ref[1:2, :], lam_ref[2:3, :], lam_ref[3:4, :]
    lam = (jnp.exp(jnp.sum(lq1 * lk1, axis=-1, keepdims=True))
           - jnp.exp(jnp.sum(lq2 * lk2, axis=-1, keepdims=True)) + lambda_init)

    def scores(i, j, slot):
        st = _dot_nt(ka_ref[j * bq:(j + 1) * bq, :], qs_ref[i])
        if j == i:
            st = st + diag_ref[0]
        st_ref[slot] = st
        mx_ref[slot] = jnp.max(st, axis=0, keepdims=True)

    def update(j, slot):
        m_old = m_ref[...]
        m_new = jnp.maximum(m_old, mx_ref[slot])
        alpha = jnp.exp2(m_old - m_new)
        p = jnp.exp2(st_ref[slot] - m_new).astype(bf16)
        acc_ref[...] = alpha * acc_ref[...] + _dot(vt_ref[:, j * bq:(j + 1) * bq], p)
        m_ref[...] = m_new

    def finalize(i):
        ot = acc_ref[:hw, :] * (1.0 / acc_ref[hw:hw + 1, :])
        o = (ot[:, :bq] - lam * ot[:, bq:]).T
        ms = jnp.mean(o * o, axis=-1, keepdims=True)
        o = o * lax.rsqrt(ms + NORM_EPS) * sub_ref[...] * one_minus_lambda_init
        o_ref[i * bq:(i + 1) * bq, :] = o.astype(o_ref.dtype)

    tiles = [(i, j) for i in range(t // bq) for j in range(i + 1)]
    scores(*tiles[0], 0)
    for n, (i, j) in enumerate(tiles):
        slot = n % 2
        if n + 1 < len(tiles):
            scores(*tiles[n + 1], 1 - slot)
        if j == 0:
            m_ref[...] = jnp.full_like(m_ref, NEG_BIG)
            acc_ref[...] = jnp.zeros_like(acc_ref)
        update(j, slot)
        if j == i:
            finalize(i)


def _attention(q, k, v, bsz, t, lam_vecs, subln_w, lambda_init):
    bq = min(ATT_BQ, t)
    nq = t // bq
    assert t <= 256 * 256, "key positions are split into two bf16-exact columns"
    slopes = jnp.exp2(-8.0 * jnp.arange(1, B_HEADS + 1, dtype=f32) / B_HEADS) * LOG2_E
    c_hi, c_mid, c_lo = (piece.astype(f32) for piece in _split3(slopes))
    bias_cols = jnp.stack([256.0 * c_hi, c_hi, 256.0 * c_mid, c_mid, 256.0 * c_lo, c_lo], axis=1)
    kk = jnp.arange(bq)[:, None]
    qq = jnp.arange(bq)[None, :]
    later = jnp.where(kk > qq, 2.0 * (qq - kk), 0.0).astype(f32)
    allowed = (kk // CHUNK) <= (qq // CHUNK)
    diag = jnp.where(allowed[None], slopes[:, None, None] * later[None], NEG_BIG)
    diag = jnp.concatenate([diag, diag], axis=2)
    hw = 2 * B_HEAD_DIM
    body = functools.partial(_attn_body, one_minus_lambda_init=1.0 - lambda_init, lambda_init=lambda_init)
    head_block = lambda: pl.BlockSpec((t, hw), lambda b, h, p: (b, h))
    return pl.pallas_call(
        body,
        grid=(bsz, B_HEADS, 2),
        in_specs=[
            pl.BlockSpec(memory_space=pltpu.SMEM),
            head_block(),
            head_block(),
            head_block(),
            pl.BlockSpec((1, bq, 2 * bq), lambda b, h, p: (h, 0, 0)),
            pl.BlockSpec((4, B_HEAD_DIM), lambda b, h, p: (0, 0)),
            pl.BlockSpec((1, hw), lambda b, h, p: (0, 0)),
        ],
        out_specs=head_block(),
        out_shape=jax.ShapeDtypeStruct((bsz * t, B_WIDTH), bf16),
        scratch_shapes=[
            pltpu.VMEM((t, 2 * hw), bf16),
            pltpu.VMEM((hw + ATT_ONES_ROWS, t), bf16),
            pltpu.VMEM((nq, 2 * bq, 2 * hw), bf16),
            pltpu.VMEM((2, bq, 2 * bq), f32),
            pltpu.VMEM((2, 1, 2 * bq), f32),
            pltpu.VMEM((1, 2 * bq), f32),
            pltpu.VMEM((hw + ATT_ONES_ROWS, 2 * bq), f32),
        ],
        compiler_params=pltpu.CompilerParams(
            dimension_semantics=("parallel", "parallel", "arbitrary"), vmem_limit_bytes=VMEM_LIMIT),
        name="diff_attn",
    )(bias_cols, q, k, v, diag, lam_vecs, subln_w.reshape(1, hw).astype(f32))


def _ffn_body(x_ref, ya_ref, yb_ref, wo_ref, nw_ref, wup_ref, cw_ref, cb_ref, wdn_ref, o_ref,
              carry_ref, act_ref, h_ref):
    tb = pl.program_id(1)
    tm = x_ref.shape[1]
    x2 = x_ref[0] + _dot(ya_ref[0], wo_ref[:A_WIDTH, :]) + _dot(yb_ref[0], wo_ref[A_WIDTH:, :])
    ms = jnp.mean(x2 * x2, axis=-1, keepdims=True)
    h_ref[...] = (x2 * lax.rsqrt(ms + NORM_EPS) * nw_ref[...]).astype(bf16)

    @pl.when(tb == 0)
    def _():
        carry_ref[...] = jnp.zeros_like(carry_ref)

    def conv(u, col):
        cols = slice(col, col + FFN_TF)
        old = carry_ref[:, cols]
        carry_ref[:, cols] = u[tm - 8:, :]
        row8 = lax.broadcasted_iota(jnp.int32, (8, 1), 0)
        u1 = pltpu.roll(u, 1, axis=0)
        u2 = pltpu.roll(u, 2, axis=0)
        u1 = jnp.concatenate([jnp.where(row8 < 1, pltpu.roll(old, 1, axis=0), u1[:8]), u1[8:]], axis=0)
        u2 = jnp.concatenate([jnp.where(row8 < 2, pltpu.roll(old, 2, axis=0), u2[:8]), u2[8:]], axis=0)
        return (cw_ref[0:1, cols] * u2 + cw_ref[1:2, cols] * u1
                + cw_ref[2:3, cols] * u + cb_ref[:, cols])

    for j in range(D_FF // FFN_TF):
        gcol = j * FFN_TF
        ucol = D_FF + j * FFN_TF
        gate = conv(_dot(h_ref[...], wup_ref[:, gcol:gcol + FFN_TF]), gcol)
        up = conv(_dot(h_ref[...], wup_ref[:, ucol:ucol + FFN_TF]), ucol)
        act_ref[:, gcol:gcol + FFN_TF] = (gate * jax.nn.sigmoid(gate) * up).astype(bf16)
    o_ref[0] = x2 + _dot(act_ref[...], wdn_ref[...])


def _out_ffn(x, ya, yb, w_out_bf, norm_w, w_up_bf, conv_w, conv_b, w_down_bf):
    bsz, t, _ = x.shape
    tm = min(FFN_TM, t)
    tok = lambda b, i: (b, i, 0)
    return pl.pallas_call(
        _ffn_body,
        grid=(bsz, t // tm),
        in_specs=[
            pl.BlockSpec((1, tm, D_MODEL), tok),
            pl.BlockSpec((1, tm, A_WIDTH), tok),
            pl.BlockSpec((1, tm, B_WIDTH), tok),
            _const_spec((A_WIDTH + B_WIDTH, D_MODEL)),
            _const_spec((1, D_MODEL)),
            _const_spec((D_MODEL, 2 * D_FF)),
            _const_spec((3, 2 * D_FF)),
            _const_spec((1, 2 * D_FF)),
            _const_spec((D_FF, D_MODEL)),
        ],
        out_specs=pl.BlockSpec((1, tm, D_MODEL), tok),
        out_shape=jax.ShapeDtypeStruct((bsz, t, D_MODEL), f32),
        scratch_shapes=[
            pltpu.VMEM((8, 2 * D_FF), f32),
            pltpu.VMEM((tm, D_FF), bf16),
            pltpu.VMEM((tm, D_MODEL), bf16),
        ],
        compiler_params=pltpu.CompilerParams(
            dimension_semantics=("parallel", "arbitrary"), vmem_limit_bytes=VMEM_LIMIT),
        name="out_ffn",
    )(x, ya, yb, w_out_bf, norm_w, w_up_bf, conv_w, conv_b, w_down_bf)


def _layer(x, lambda_init, attn_norm_w, w_in, mu_shift, w0, w_decay_up, a0, w_aaa_up, w_gate_up,
           k_k, k_a, r_k, ln_x_w, ln_x_b, q_norm_w, k_norm_w, lambda_q1, lambda_k1, lambda_q2,
           lambda_k2, subln_w, w_out, ffn_norm_w, w_ffn_up, ffn_conv_w, ffn_conv_b, w_ffn_down):
    bsz, t, _ = x.shape
    head_ids = jnp.arange(MXU_DIM) // A_HEAD_DIM
    ones_bd = (head_ids[:, None] == head_ids[None, :]).astype(bf16)
    scale = LOG2_E / math.sqrt(B_HEAD_DIM)
    qk_w = jnp.stack([jnp.tile(q_norm_w.astype(f32), B_WIDTH // B_HEAD_DIM) * scale,
                      jnp.tile(k_norm_w.astype(f32), B_WIDTH // B_HEAD_DIM)])
    za, q, k, v = _inproj(x.reshape(bsz * t, D_MODEL), attn_norm_w.reshape(1, D_MODEL),
                          w_in.astype(bf16), qk_w, ones_bd)

    zeros = jnp.zeros((DECAY_LORA, A_WIDTH), f32)
    w_lora = jnp.concatenate([jnp.concatenate([w_decay_up, zeros], axis=1),
                              jnp.concatenate([zeros, w_aaa_up], axis=1)], axis=0).astype(bf16)
    ya = _rwkv(za.reshape(bsz, t, A_IN), mu_shift, w0, a0, w_lora, w_gate_up.astype(bf16),
               k_k, k_a, r_k, ln_x_w, ln_x_b, ones_bd)

    lam_vecs = jnp.stack([lambda_q1, lambda_k1, lambda_q2, lambda_k2]).astype(f32)
    yb = _attention(q, k, v, bsz, t, lam_vecs, subln_w, lambda_init)

    return _out_ffn(x, ya, yb.reshape(bsz, t, B_WIDTH), w_out.astype(bf16),
                    ffn_norm_w.reshape(1, D_MODEL), w_ffn_up.astype(bf16), ffn_conv_w,
                    ffn_conv_b.reshape(1, 2 * D_FF), w_ffn_down.astype(bf16))


def kernel(x, attn_norm_w, w_in, mu_shift, w0, w_decay_up, a0, w_aaa_up, w_gate_up, k_k, k_a, r_k,
           ln_x_w, ln_x_b, q_norm_w, k_norm_w, lambda_q1, lambda_k1, lambda_q2, lambda_k2, subln_w,
           w_out, ffn_norm_w, w_ffn_up, ffn_conv_w, ffn_conv_b, w_ffn_down):
    params = (attn_norm_w, w_in, mu_shift, w0, w_decay_up, a0, w_aaa_up, w_gate_up, k_k, k_a, r_k,
              ln_x_w, ln_x_b, q_norm_w, k_norm_w, lambda_q1, lambda_k1, lambda_q2, lambda_k2,
              subln_w, w_out, ffn_norm_w, w_ffn_up, ffn_conv_w, ffn_conv_b, w_ffn_down)
    for l in range(attn_norm_w.shape[0]):
        lambda_init = 0.8 - 0.6 * math.exp(-0.3 * l)
        x = _layer(x, lambda_init, *(p[l] for p in params))
    return x
```

```python
import functools
import math

import jax
import jax.numpy as jnp
from jax import lax
from jax.experimental import pallas as pl
from jax.experimental.pallas import tpu as pltpu

f32 = jnp.float32
bf16 = jnp.bfloat16

D_MODEL = 1024
CHUNK = 64
A_HEADS = 8
A_HEAD_DIM = 64
A_WIDTH = A_HEADS * A_HEAD_DIM
DECAY_LORA = 64
AAA_LORA = 64
GATE_LORA = 128
A_IN = 3 * A_WIDTH + DECAY_LORA + AAA_LORA + GATE_LORA
B_HEADS = 4
B_HEAD_DIM = 64
B_WIDTH = B_HEADS * 2 * B_HEAD_DIM
B_IN = 3 * B_WIDTH
IN_WIDTH = A_IN + B_IN
D_FF = 2816
NORM_EPS = 1e-6
LNX_EPS = 64e-5
DECAY_SCALE = math.exp(-0.5)
L2_EPS = 1e-12

LANES = 128
MXU_DIM = 256
VMEM_LIMIT = 56 * 1024 * 1024

IN_TM = 512
SCAN_T = 256
ATT_BQ = 512
ATT_ONES_ROWS = 16
LOG2_E = math.log2(math.e)
FFN_TM = 512
FFN_TF = 256
NEG_BIG = -1e30


def _dot(a, b, dims=((1,), (0,))):
    return lax.dot_general(a, b, (dims, ((), ())), preferred_element_type=f32)


def _dot_nt(a, b):
    return _dot(a, b, ((1,), (1,)))


def _split2(x):
    hi = x.astype(bf16)
    lo = (x - hi.astype(f32)).astype(bf16)
    return hi, lo


def _split3(x):
    hi = x.astype(bf16)
    r1 = x - hi.astype(f32)
    mid = r1.astype(bf16)
    lo = (r1 - mid.astype(f32)).astype(bf16)
    return hi, mid, lo


def _group_sum(x, ones_bd):
    xb = x.astype(bf16)
    parts = []
    for g in range(x.shape[1] // MXU_DIM):
        cols = slice(g * MXU_DIM, (g + 1) * MXU_DIM)
        parts.append(_dot(xb[:, cols], ones_bd))
    return jnp.concatenate(parts, axis=1)


def _const_spec(shape):
    nd = len(shape)
    return pl.BlockSpec(shape, lambda *_: (0,) * nd, pipeline_mode=pl.Buffered(1))


def _inproj_body(x_ref, nw_ref, w_ref, qkw_ref, ones_ref, za_ref, q_ref, k_ref, v_ref):
    x = x_ref[...]
    ms = jnp.mean(x * x, axis=-1, keepdims=True)
    h = (x * lax.rsqrt(ms + NORM_EPS) * nw_ref[...]).astype(bf16)
    za_ref[...] = _dot(h, w_ref[:, :A_IN])
    ones_bd = ones_ref[...]

    def qk_norm(col0, row):
        z = _dot(h, w_ref[:, col0:col0 + B_WIDTH])
        ss = _group_sum(z * z, ones_bd)
        return (z * lax.rsqrt(ss * (1.0 / B_HEAD_DIM) + NORM_EPS) * qkw_ref[row:row + 1, :]).astype(bf16)

    q_ref[...] = qk_norm(A_IN, 0)
    k_ref[...] = qk_norm(A_IN + B_WIDTH, 1)
    v_ref[...] = _dot(h, w_ref[:, A_IN + 2 * B_WIDTH:]).astype(bf16)


def _inproj(x2d, norm_w, w_in_bf, qk_w, ones_bd):
    n = x2d.shape[0]
    tm = min(IN_TM, n)
    row = lambda i: (i, 0)
    return pl.pallas_call(
        _inproj_body,
        grid=(n // tm,),
        in_specs=[
            pl.BlockSpec((tm, D_MODEL), row),
            _const_spec((1, D_MODEL)),
            _const_spec((D_MODEL, IN_WIDTH)),
            _const_spec((2, B_WIDTH)),
            _const_spec((MXU_DIM, MXU_DIM)),
        ],
        out_specs=[
            pl.BlockSpec((tm, A_IN), row),
            pl.BlockSpec((tm, B_WIDTH), row),
            pl.BlockSpec((tm, B_WIDTH), row),
            pl.BlockSpec((tm, B_WIDTH), row),
        ],
        out_shape=[
            jax.ShapeDtypeStruct((n, A_IN), f32),
            jax.ShapeDtypeStruct((n, B_WIDTH), bf16),
            jax.ShapeDtypeStruct((n, B_WIDTH), bf16),
            jax.ShapeDtypeStruct((n, B_WIDTH), bf16),
        ],
        compiler_params=pltpu.CompilerParams(
            dimension_semantics=("parallel",), vmem_limit_bytes=VMEM_LIMIT),
        name="inproj",
    )(x2d, norm_w, w_in_bf, qk_w, ones_bd)


def _rwkv_body(za_ref, halo_ref, mu_ref, w0_ref, a0_ref, wlora_ref, wgate_ref, kk_ref, ka_ref,
               rk_ref, lnw_ref, lnb_ref, ones_ref, out_ref, s_ref):
    tb = pl.program_id(1)
    tc = za_ref.shape[1]
    n_chunks = tc // CHUNK

    @pl.when(tb == 0)
    def _():
        s_ref[...] = jnp.zeros_like(s_ref)

    h = za_ref[0]
    last = jnp.where(tb == 0, 0.0, halo_ref[0][7:8, :])
    row0 = lax.broadcasted_iota(jnp.int32, (tc, 1), 0) == 0
    prev = jnp.where(row0, last, pltpu.roll(h, 1, axis=0))
    hs = h + (prev - h) * mu_ref[...]

    r = hs[:, 0:A_WIDTH]
    k = hs[:, A_WIDTH:2 * A_WIDTH]
    v = hs[:, 2 * A_WIDTH:3 * A_WIDTH]
    lora_in = hs[:, 3 * A_WIDTH:3 * A_WIDTH + LANES]
    gate_in = hs[:, 3 * A_WIDTH + LANES:]
    lane = lax.broadcasted_iota(jnp.int32, (1, LANES), 1)
    lora_act = jnp.where(lane < DECAY_LORA, jnp.tanh(lora_in), lora_in).astype(bf16)
    lora = _dot(lora_act, wlora_ref[...])
    logw = -DECAY_SCALE * jax.nn.sigmoid(w0_ref[...] + lora[:, :A_WIDTH])
    lr = jax.nn.sigmoid(a0_ref[...] + lora[:, A_WIDTH:])
    gate = _dot(jax.nn.sigmoid(gate_in).astype(bf16), wgate_ref[...])

    ones_bd = ones_ref[...]
    kk = k * kk_ref[...]
    kk = kk * lax.rsqrt(_group_sum(kk * kk, ones_bd) + L2_EPS)
    kh = k * (1.0 + (lr - 1.0) * ka_ref[...])
    a_s = -kk
    b_s = kk * lr
    bonus = _group_sum(r * kh * rk_ref[...], ones_bd) * v

    ri = lax.broadcasted_iota(jnp.int32, (tc, tc), 0)
    ci = lax.broadcasted_iota(jnp.int32, (tc, tc), 1)
    in_chunk = (ri & (CHUNK - 1)).astype(jnp.uint32)
    dist = (ri - ci).astype(jnp.uint32)
    incl = dist <= in_chunk
    strict = (dist - 1) < in_chunk
    tri = jnp.where(incl, 1.0, 0.0).astype(bf16)
    w_hi, w_lo = _split2(logw)
    cs = _dot(tri, w_hi) + _dot(tri, w_lo)
    cs3 = cs.reshape(n_chunks, CHUNK, A_WIDTH)
    cs_end = jnp.broadcast_to(cs3[:, CHUNK - 1:CHUNK, :], cs3.shape).reshape(tc, A_WIDTH)

    e_neg = jnp.exp(-cs)
    e_end = jnp.exp(cs_end - cs)
    a_t = a_s * jnp.exp(cs - logw)
    r_t = r * jnp.exp(cs)
    b_t = b_s * e_neg
    k_t = kh * e_neg
    b_h = b_s * e_end
    k_h = kh * e_end
    g_end = jnp.exp(cs_end)

    heads = []
    for hd in range(A_HEADS):
        sl = slice((hd // 2) * LANES, (hd // 2 + 1) * LANES)
        lo = (hd % 2) * A_HEAD_DIM
        hm = (lane >= lo) & (lane < lo + A_HEAD_DIM)
        a_m = jnp.where(hm, a_t[:, sl], 0.0)
        r_m = jnp.where(hm, r_t[:, sl], 0.0)
        v_mb = jnp.where(hm, v[:, sl], 0.0).astype(bf16)
        heads.append(dict(sl=sl, a_m=a_m, r_m=r_m, v_mb=v_mb))
    gi = lax.broadcasted_iota(jnp.int32, (MXU_DIM, MXU_DIM), 0)
    gj = lax.broadcasted_iota(jnp.int32, (MXU_DIM, MXU_DIM), 1)
    own_lanes = (gi ^ gj) < A_HEAD_DIM
    per_group = MXU_DIM // A_HEAD_DIM
    slabs = {}
    for g in range(A_WIDTH // MXU_DIM):
        gs = slice(g * MXU_DIM, (g + 1) * MXU_DIM)
        for c in range(n_chunks):
            rows = slice(c * CHUNK, (c + 1) * CHUNK)
            ar = jnp.concatenate([a_t[rows, gs], r_t[rows, gs]], axis=0).astype(bf16)
            b4 = jnp.where(own_lanes, jnp.concatenate([b_t[rows, gs]] * per_group, axis=0), 0.0)
            k4 = jnp.where(own_lanes, jnp.concatenate([k_t[rows, gs]] * per_group, axis=0), 0.0)
            slabs[g, c] = _dot_nt(ar, jnp.concatenate([b4, k4], axis=0).astype(bf16))
    for hd, st in enumerate(heads):
        g, hl = divmod(hd, per_group)

        def block_diag(row0, col0):
            parts = []
            for c in range(n_chunks):
                slab = slabs[g, c][row0:row0 + CHUNK, col0:col0 + MXU_DIM]
                shift = ((c - hl) * A_HEAD_DIM) % MXU_DIM
                parts.append(pltpu.roll(slab, shift, axis=1) if shift else slab)
            return jnp.concatenate(parts, axis=0)

        st["l_ab"] = jnp.where(strict, block_diag(0, 0), 0.0).astype(bf16)
        st["l_ak"] = jnp.where(strict, block_diag(0, MXU_DIM), 0.0).astype(bf16)
        st["m_rb"] = jnp.where(incl, block_diag(CHUNK, 0), 0.0).astype(bf16)
        st["m_rk"] = jnp.where(incl, block_diag(CHUNK, MXU_DIM), 0.0).astype(bf16)
    bi = lax.broadcasted_iota(jnp.int32, (CHUNK, MXU_DIM), 0)
    bj = lax.broadcasted_iota(jnp.int32, (CHUNK, MXU_DIM), 1) & (A_HEAD_DIM - 1)
    kv_parts = {}
    for g in range(A_WIDTH // MXU_DIM):
        gs = slice(g * MXU_DIM, (g + 1) * MXU_DIM)
        for c in range(n_chunks):
            rows = slice(c * CHUNK, (c + 1) * CHUNK)
            sl_c = slabs[g, c]
            lhs = jnp.concatenate([jnp.where(bj < bi, sl_c[:CHUNK, MXU_DIM:], 0.0),
                                   jnp.where(bj <= bi, sl_c[CHUNK:, MXU_DIM:], 0.0)], axis=0).astype(bf16)
            v4 = jnp.where(own_lanes, jnp.concatenate([v[rows, gs]] * per_group, axis=0), 0.0).astype(bf16)
            kv_parts[g, c] = _dot(lhs, v4)
    for hd, st in enumerate(heads):
        g, hl = divmod(hd, per_group)
        pl_ = slice((hl // 2) * LANES, (hl // 2 + 1) * LANES)
        lo = (hd % 2) * A_HEAD_DIM
        hm = (lane >= lo) & (lane < lo + A_HEAD_DIM)
        akv = jnp.concatenate([kv_parts[g, c][:CHUNK, pl_] for c in range(n_chunks)], axis=0)
        rkv = jnp.concatenate([kv_parts[g, c][CHUNK:, pl_] for c in range(n_chunks)], axis=0)
        st["rkv"] = jnp.where(hm, rkv, 0.0)
        st["x"] = jnp.concatenate([st["a_m"], jnp.where(hm, akv, 0.0)], axis=1)
    for st in heads:
        st["p"] = st["l_ab"]
        st["x"] = st["x"] + _dot(st["p"], st["x"].astype(bf16))
    for _ in range(5):
        for st in heads:
            st["p"] = _dot(st["p"], st["p"]).astype(bf16)
        for st in heads:
            st["x"] = st["x"] + _dot(st["p"], st["x"].astype(bf16))
    for st in heads:
        st["w"] = st["x"][:, :LANES]
        st["u"] = st["x"][:, LANES:]
        rb_wu = _dot(st["m_rb"], st["x"].astype(bf16))
        st["rp"] = st["r_m"] + rb_wu[:, :LANES]
        st["yi"] = rb_wu[:, LANES:] + st["rkv"]

    qi = lax.broadcasted_iota(jnp.int32, (MXU_DIM, MXU_DIM), 0)
    qj = lax.broadcasted_iota(jnp.int32, (MXU_DIM, MXU_DIM), 1)
    same_head = (qi ^ qj) < A_HEAD_DIM
    n_groups = A_WIDTH // MXU_DIM
    per_group = A_HEADS // n_groups

    def group_cat(key, g):
        hs_g = heads[g * per_group:(g + 1) * per_group]
        pairs = [hs_g[i][key] + hs_g[i + 1][key] for i in range(0, per_group, 2)]
        return jnp.concatenate(pairs, axis=1)

    groups = []
    for g in range(n_groups):
        gs = slice(g * MXU_DIM, (g + 1) * MXU_DIM)
        groups.append(dict(gs=gs, w=group_cat("w", g), u=group_cat("u", g), rp=group_cat("rp", g),
                           yi=group_cat("yi", g), pt=[], qt=[], y=[]))
    for c in range(n_chunks):
        rows = slice(c * CHUNK, (c + 1) * CHUNK)
        for gr in groups:
            gs = gr["gs"]
            bh_c = b_h[rows, gs]
            gr["pt"].append(jnp.where(same_head, _dot(gr["w"][rows].T.astype(bf16), bh_c.astype(bf16)),
                                      0.0).astype(bf16))
            uv = jnp.concatenate([gr["u"][rows], v[rows, gs]], axis=0)
            bkh = jnp.concatenate([bh_c, k_h[rows, gs]], axis=0)
            gr["qt"].append(jnp.where(same_head, _dot(uv.T.astype(bf16), bkh.astype(bf16)), 0.0))
    states = [s_ref[g] for g in range(n_groups)]
    for c in range(n_chunks):
        rows = slice(c * CHUNK, (c + 1) * CHUNK)
        for g, gr in enumerate(groups):
            s_b = states[g].astype(bf16)
            gr["y"].append(_dot_nt(gr["rp"][rows].astype(bf16), s_b) + gr["yi"][rows])
            states[g] = (states[g] * g_end[c * CHUNK:c * CHUNK + 1, gr["gs"]]
                         + _dot(s_b, gr["pt"][c]) + gr["qt"][c])
    for g in range(n_groups):
        s_ref[g] = states[g]
    y = jnp.concatenate([jnp.concatenate(gr["y"], axis=0) for gr in groups], axis=1)

    inv_n = 1.0 / A_HEAD_DIM
    mean = _group_sum(y, ones_bd) * inv_n
    d = y - mean
    var = _group_sum(d * d, ones_bd) * inv_n
    yn = d * lax.rsqrt(var + LNX_EPS) * lnw_ref[...] + lnb_ref[...]
    out_ref[0] = ((yn + bonus) * gate).astype(out_ref.dtype)


def _rwkv(za, mu, w0, a0, w_lora, w_gate, k_k, k_a, r_k, ln_w, ln_b, ones_bd):
    bsz, t, _ = za.shape
    tc = min(SCAN_T, t)
    halo_rows = 8
    per = tc // halo_rows
    vec = lambda a: a.reshape(1, -1).astype(f32)
    return pl.pallas_call(
        _rwkv_body,
        grid=(bsz, t // tc),
        in_specs=[
            pl.BlockSpec((1, tc, A_IN), lambda b, i: (b, i, 0)),
            pl.BlockSpec((1, halo_rows, A_IN), lambda b, i: (b, jnp.maximum(i * per - 1, 0), 0)),
            _const_spec((1, A_IN)),
            _const_spec((1, A_WIDTH)),
            _const_spec((1, A_WIDTH)),
            _const_spec((LANES, 2 * A_WIDTH)),
            _const_spec((GATE_LORA, A_WIDTH)),
            _const_spec((1, A_WIDTH)),
            _const_spec((1, A_WIDTH)),
            _const_spec((1, A_WIDTH)),
            _const_spec((1, A_WIDTH)),
            _const_spec((1, A_WIDTH)),
            _const_spec((MXU_DIM, MXU_DIM)),
        ],
        out_specs=pl.BlockSpec((1, tc, A_WIDTH), lambda b, i: (b, i, 0)),
        out_shape=jax.ShapeDtypeStruct((bsz, t, A_WIDTH), bf16),
        scratch_shapes=[pltpu.VMEM((A_WIDTH // MXU_DIM, MXU_DIM, MXU_DIM), f32)],
        compiler_params=pltpu.CompilerParams(
            dimension_semantics=("parallel", "arbitrary"), vmem_limit_bytes=VMEM_LIMIT),
        name="rwkv7",
    )(za, za, vec(mu), vec(w0), vec(a0), w_lora, w_gate, vec(k_k), vec(k_a), vec(r_k),
      vec(ln_w), vec(ln_b), ones_bd)


def _attn_body(bias_ref, q_ref, k_ref, v_ref, diag_ref, lam_ref, sub_ref, o_ref,
               ka_ref, vt_ref, qs_ref, st_ref, mx_ref, m_ref, acc_ref, *, one_minus_lambda_init, lambda_init):
    hd = pl.program_id(1)
    phase = pl.program_id(2)
    t = k_ref.shape[0]
    bq = diag_ref.shape[1]
    hw = 2 * B_HEAD_DIM
    n_bias = bias_ref.shape[1]
    lane = lax.broadcasted_iota(jnp.int32, (1, hw), 1)

    @pl.when(phase == 0)
    def _():
        pos = lax.broadcasted_iota(jnp.int32, (t, hw), 0)
        col = lax.broadcasted_iota(jnp.int32, (t, hw), 1)
        pos_cols = jnp.where(col < n_bias, jnp.where((col & 1) == 0, pos >> 8, pos & 255), 0)
        ka_ref[:, :hw] = k_ref[...]
        ka_ref[:, hw:] = pos_cols.astype(f32).astype(bf16)
        vt_ref[:hw, :] = v_ref[...].astype(f32).T.astype(bf16)
        ones_row = lax.broadcasted_iota(jnp.int32, (ATT_ONES_ROWS, t), 0) == 0
        vt_ref[hw:, :] = jnp.where(ones_row, 1.0, 0.0).astype(bf16)
        q_pos = jnp.zeros((1, hw), f32)
        for i in range(n_bias):
            q_pos = jnp.where(lane == i, bias_ref[hd, i], q_pos)
        q_pos = jnp.broadcast_to(q_pos, (bq, hw)).astype(bf16)
        for i in range(t // bq):
            q = q_ref[i * bq:(i + 1) * bq, :]
            zero = jnp.zeros_like(q)
            qs_ref[i, :bq, :hw] = jnp.where(lane < B_HEAD_DIM, q, zero)
            qs_ref[i, bq:, :hw] = jnp.where(lane >= B_HEAD_DIM, q, zero)
            qs_ref[i, :bq, hw:] = q_pos
            qs_ref[i, bq:, hw:] = q_pos

    @pl.when(phase == 1)
    def _():
        _attn_tiles(q_ref, diag_ref, lam_ref, sub_ref, o_ref, ka_ref, vt_ref, qs_ref, st_ref, mx_ref,
                    m_ref, acc_ref, one_minus_lambda_init, lambda_init)


def _attn_tiles(q_ref, diag_ref, lam_ref, sub_ref, o_ref, ka_ref, vt_ref, qs_ref, st_ref, mx_ref,
                m_ref, acc_ref, one_minus_lambda_init, lambda_init):
    t = ka_ref.shape[0]
    bq = diag_ref.shape[1]
    hw = 2 * B_HEAD_DIM

    lq1, lk1, lq2, lk2 = lam_ref[0:1, :], lam_ref[1:2, :], lam_ref[2:3, :], lam_ref[3:4, :]
    lam = (jnp.exp(jnp.sum(lq1 * lk1, axis=-1, keepdims=True))
           - jnp.exp(jnp.sum(lq2 * lk2, axis=-1, keepdims=True)) + lambda_init)

    def scores(i, j, slot):
        st = _dot_nt(ka_ref[j * bq:(j + 1) * bq, :], qs_ref[i])
        if j == i:
            st = st + diag_ref[0]
        st_ref[slot] = st
        mx_ref[slot] = jnp.max(st, axis=0, keepdims=True)

    def update(j, slot):
        m_old = m_ref[...]
        m_new = jnp.maximum(m_old, mx_ref[slot])
        alpha = jnp.exp2(m_old - m_new)
        p = jnp.exp2(st_ref[slot] - m_new).astype(bf16)
        acc_ref[...] = alpha * acc_ref[...] + _dot(vt_ref[:, j * bq:(j + 1) * bq], p)
        m_ref[...] = m_new

    def finalize(i):
        ot = acc_ref[:hw, :] * (1.0 / acc_ref[hw:hw + 1, :])
        o = (ot[:, :bq] - lam * ot[:, bq:]).T
        ms = jnp.mean(o * o, axis=-1, keepdims=True)
        o = o * lax.rsqrt(ms + NORM_EPS) * sub_ref[...] * one_minus_lambda_init
        o_ref[i * bq:(i + 1) * bq, :] = o.astype(o_ref.dtype)

    tiles = [(i, j) for i in range(t // bq) for j in range(i + 1)]
    scores(*tiles[0], 0)
    for n, (i, j) in enumerate(tiles):
        slot = n % 2
        if n + 1 < len(tiles):
            scores(*tiles[n + 1], 1 - slot)
        if j == 0:
            m_ref[...] = jnp.full_like(m_ref, NEG_BIG)
            acc_ref[...] = jnp.zeros_like(acc_ref)
        update(j, slot)
        if j == i:
            finalize(i)


def _attention(q, k, v, bsz, t, lam_vecs, subln_w, lambda_init):
    bq = min(ATT_BQ, t)
    nq = t // bq
    assert t <= 256 * 256, "key positions are split into two bf16-exact columns"
    slopes = jnp.exp2(-8.0 * jnp.arange(1, B_HEADS + 1, dtype=f32) / B_HEADS) * LOG2_E
    c_hi, c_mid, c_lo = (piece.astype(f32) for piece in _split3(slopes))
    bias_cols = jnp.stack([256.0 * c_hi, c_hi, 256.0 * c_mid, c_mid, 256.0 * c_lo, c_lo], axis=1)
    kk = jnp.arange(bq)[:, None]
    qq = jnp.arange(bq)[None, :]
    later = jnp.where(kk > qq, 2.0 * (qq - kk), 0.0).astype(f32)
    allowed = (kk // CHUNK) <= (qq // CHUNK)
    diag = jnp.where(allowed[None], slopes[:, None, None] * later[None], NEG_BIG)
    diag = jnp.concatenate([diag, diag], axis=2)
    hw = 2 * B_HEAD_DIM
    body = functools.partial(_attn_body, one_minus_lambda_init=1.0 - lambda_init, lambda_init=lambda_init)
    head_block = lambda: pl.BlockSpec((t, hw), lambda b, h, p: (b, h))
    return pl.pallas_call(
        body,
        grid=(bsz, B_HEADS, 2),
        in_specs=[
            pl.BlockSpec(memory_space=pltpu.SMEM),
            head_block(),
            head_block(),
            head_block(),
            pl.BlockSpec((1, bq, 2 * bq), lambda b, h, p: (h, 0, 0)),
            pl.BlockSpec((4, B_HEAD_DIM), lambda b, h, p: (0, 0)),
            pl.BlockSpec((1, hw), lambda b, h, p: (0, 0)),
        ],
        out_specs=head_block(),
        out_shape=jax.ShapeDtypeStruct((bsz * t, B_WIDTH), bf16),
        scratch_shapes=[
            pltpu.VMEM((t, 2 * hw), bf16),
            pltpu.VMEM((hw + ATT_ONES_ROWS, t), bf16),
            pltpu.VMEM((nq, 2 * bq, 2 * hw), bf16),
            pltpu.VMEM((2, bq, 2 * bq), f32),
            pltpu.VMEM((2, 1, 2 * bq), f32),
            pltpu.VMEM((1, 2 * bq), f32),
            pltpu.VMEM((hw + ATT_ONES_ROWS, 2 * bq), f32),
        ],
        compiler_params=pltpu.CompilerParams(
            dimension_semantics=("parallel", "parallel", "arbitrary"), vmem_limit_bytes=VMEM_LIMIT),
        name="diff_attn",
    )(bias_cols, q, k, v, diag, lam_vecs, subln_w.reshape(1, hw).astype(f32))


def _ffn_body(x_ref, ya_ref, yb_ref, wo_ref, nw_ref, wup_ref, cw_ref, cb_ref, wdn_ref, o_ref,
              carry_ref, act_ref, h_ref):
    tb = pl.program_id(1)
    tm = x_ref.shape[1]
    x2 = x_ref[0] + _dot(ya_ref[0], wo_ref[:A_WIDTH, :]) + _dot(yb_ref[0], wo_ref[A_WIDTH:, :])
    ms = jnp.mean(x2 * x2, axis=-1, keepdims=True)
    h_ref[...] = (x2 * lax.rsqrt(ms + NORM_EPS) * nw_ref[...]).astype(bf16)

    @pl.when(tb == 0)
    def _():
        carry_ref[...] = jnp.zeros_like(carry_ref)

    def conv(u, col):
        cols = slice(col, col + FFN_TF)
        old = carry_ref[:, cols]
        carry_ref[:, cols] = u[tm - 8:, :]
        row8 = lax.broadcasted_iota(jnp.int32, (8, 1), 0)
        u1 = pltpu.roll(u, 1, axis=0)
        u2 = pltpu.roll(u, 2, axis=0)
        u1 = jnp.concatenate([jnp.where(row8 < 1, pltpu.roll(old, 1, axis=0), u1[:8]), u1[8:]], axis=0)
        u2 = jnp.concatenate([jnp.where(row8 < 2, pltpu.roll(old, 2, axis=0), u2[:8]), u2[8:]], axis=0)
        return (cw_ref[0:1, cols] * u2 + cw_ref[1:2, cols] * u1
                + cw_ref[2:3, cols] * u + cb_ref[:, cols])

    for j in range(D_FF // FFN_TF):
        gcol = j * FFN_TF
        ucol = D_FF + j * FFN_TF
        gate = conv(_dot(h_ref[...], wup_ref[:, gcol:gcol + FFN_TF]), gcol)
        up = conv(_dot(h_ref[...], wup_ref[:, ucol:ucol + FFN_TF]), ucol)
        act_ref[:, gcol:gcol + FFN_TF] = (gate * jax.nn.sigmoid(gate) * up).astype(bf16)
    o_ref[0] = x2 + _dot(act_ref[...], wdn_ref[...])


def _out_ffn(x, ya, yb, w_out_bf, norm_w, w_up_bf, conv_w, conv_b, w_down_bf):
    bsz, t, _ = x.shape
    tm = min(FFN_TM, t)
    tok = lambda b, i: (b, i, 0)
    return pl.pallas_call(
        _ffn_body,
        grid=(bsz, t // tm),
        in_specs=[
            pl.BlockSpec((1, tm, D_MODEL), tok),
            pl.BlockSpec((1, tm, A_WIDTH), tok),
            pl.BlockSpec((1, tm, B_WIDTH), tok),
            _const_spec((A_WIDTH + B_WIDTH, D_MODEL)),
            _const_spec((1, D_MODEL)),
            _const_spec((D_MODEL, 2 * D_FF)),
            _const_spec((3, 2 * D_FF)),
            _const_spec((1, 2 * D_FF)),
            _const_spec((D_FF, D_MODEL)),
        ],
        out_specs=pl.BlockSpec((1, tm, D_MODEL), tok),
        out_shape=jax.ShapeDtypeStruct((bsz, t, D_MODEL), f32),
        scratch_shapes=[
            pltpu.VMEM((8, 2 * D_FF), f32),
            pltpu.VMEM((tm, D_FF), bf16),
            pltpu.VMEM((tm, D_MODEL), bf16),
        ],
        compiler_params=pltpu.CompilerParams(
            dimension_semantics=("parallel", "arbitrary"), vmem_limit_bytes=VMEM_LIMIT),
        name="out_ffn",
    )(x, ya, yb, w_out_bf, norm_w, w_up_bf, conv_w, conv_b, w_down_bf)


def _layer(x, lambda_init, attn_norm_w, w_in, mu_shift, w0, w_decay_up, a0, w_aaa_up, w_gate_up,
           k_k, k_a, r_k, ln_x_w, ln_x_b, q_norm_w, k_norm_w, lambda_q1, lambda_k1, lambda_q2,
           lambda_k2, subln_w, w_out, ffn_norm_w, w_ffn_up, ffn_conv_w, ffn_conv_b, w_ffn_down):
    bsz, t, _ = x.shape
    head_ids = jnp.arange(MXU_DIM) // A_HEAD_DIM
    ones_bd = (head_ids[:, None] == head_ids[None, :]).astype(bf16)
    scale = LOG2_E / math.sqrt(B_HEAD_DIM)
    qk_w = jnp.stack([jnp.tile(q_norm_w.astype(f32), B_WIDTH // B_HEAD_DIM) * scale,
                      jnp.tile(k_norm_w.astype(f32), B_WIDTH // B_HEAD_DIM)])
    za, q, k, v = _inproj(x.reshape(bsz * t, D_MODEL), attn_norm_w.reshape(1, D_MODEL),
                          w_in.astype(bf16), qk_w, ones_bd)

    zeros = jnp.zeros((DECAY_LORA, A_WIDTH), f32)
    w_lora = jnp.concatenate([jnp.concatenate([w_decay_up, zeros], axis=1),
                              jnp.concatenate([zeros, w_aaa_up], axis=1)], axis=0).astype(bf16)
    ya = _rwkv(za.reshape(bsz, t, A_IN), mu_shift, w0, a0, w_lora, w_gate_up.astype(bf16),
               k_k, k_a, r_k, ln_x_w, ln_x_b, ones_bd)

    lam_vecs = jnp.stack([lambda_q1, lambda_k1, lambda_q2, lambda_k2]).astype(f32)
    yb = _attention(q, k, v, bsz, t, lam_vecs, subln_w, lambda_init)

    return _out_ffn(x, ya, yb.reshape(bsz, t, B_WIDTH), w_out.astype(bf16),
                    ffn_norm_w.reshape(1, D_MODEL), w_ffn_up.astype(bf16), ffn_conv_w,
                    ffn_conv_b.reshape(1, 2 * D_FF), w_ffn_down.astype(bf16))


def kernel(x, attn_norm_w, w_in, mu_shift, w0, w_decay_up, a0, w_aaa_up, w_gate_up, k_k, k_a, r_k,
           ln_x_w, ln_x_b, q_norm_w, k_norm_w, lambda_q1, lambda_k1, lambda_q2, lambda_k2, subln_w,
           w_out, ffn_norm_w, w_ffn_up, ffn_conv_w, ffn_conv_b, w_ffn_down):
    params = (attn_norm_w, w_in, mu_shift, w0, w_decay_up, a0, w_aaa_up, w_gate_up, k_k, k_a, r_k,
              ln_x_w, ln_x_b, q_norm_w, k_norm_w, lambda_q1, lambda_k1, lambda_q2, lambda_k2,
              subln_w, w_out, ffn_norm_w, w_ffn_up, ffn_conv_w, ffn_conv_b, w_ffn_down)
    for l in range(attn_norm_w.shape[0]):
        lambda_init = 0.8 - 0.6 * math.exp(-0.3 * l)
        x = _layer(x, lambda_init, *(p[l] for p in params))
    return x
```
